```python
import math
import jax, jax.numpy as jnp
from jax import lax
import numpy as np

D_MODEL = 1024
BATCH = 4
SEQ = 4096
DEPTH = 2
DEC_BATCH = 8
DEC_SEQ = 4096
PAST_LEN = 128

HEAD_DIM = 64
N_HEADS_A = 8
WIDTH_A = N_HEADS_A * HEAD_DIM
DIL_PATTERNS = ((128, 1), (512, 4), (2048, 16))
N_HEADS_B = 4
QK_WIDTH_B = N_HEADS_B * 2 * HEAD_DIM
V_DIM_B = 2 * HEAD_DIM
WIDTH_B = N_HEADS_B * V_DIM_B
MIX_WIDTH = WIDTH_A + WIDTH_B
IN_COLS = 3 * WIDTH_A + 2 * QK_WIDTH_B + WIDTH_B
ROPE_THETA = 10000.0
EPS = 1e-6
Q_BLOCK = 128
N_GROUPS = 4
EXPERTS_PER_GROUP = 8
N_EXPERTS = N_GROUPS * EXPERTS_PER_GROUP
TOP_K_INNER = 2
D_FF_EXPERT = 512
MOE_BLOCK = 128
NEG = -1e30

kernel_name = "hybrid_dilated_diff_hmoe_encoder"


def _rmsnorm(x, g):
    xf = x.astype(jnp.float32)
    y = xf * lax.rsqrt(jnp.mean(xf * xf, axis=-1, keepdims=True) + EPS)
    return (y * g.astype(jnp.float32)).astype(x.dtype)


def _rope(x, pos):
    half = x.shape[-1] // 2
    inv = ROPE_THETA ** (-jnp.arange(half, dtype=jnp.float32) / half)
    ang = pos[:, None] * inv[None, :]
    cos = jnp.cos(ang)[None, :, None, :]
    sin = jnp.sin(ang)[None, :, None, :]
    xf = x.astype(jnp.float32)
    x1, x2 = xf[..., :half], xf[..., half:]
    return jnp.concatenate([x1 * cos - x2 * sin, x2 * cos + x1 * sin], axis=-1).astype(x.dtype)


def _band_attention(q, k, v, radius):
    N, L, H, dh = q.shape
    c = radius
    nc = -(-L // c)
    Lp = nc * c
    pad = Lp - L
    qp = jnp.pad(q, ((0, 0), (0, pad), (0, 0), (0, 0))).reshape(N, nc, c, H, dh)
    kp = jnp.pad(k, ((0, 0), (c, pad + c), (0, 0), (0, 0)))
    vp = jnp.pad(v, ((0, 0), (c, pad + c), (0, 0), (0, 0)))

    def neighbours(t):
        return jnp.concatenate([t[:, j * c:j * c + Lp].reshape(N, nc, c, H, dh) for j in range(3)], axis=2)

    kc, vc = neighbours(kp), neighbours(vp)
    t = jnp.arange(c)[:, None]
    u = jnp.arange(3 * c)[None, :]
    rel = u - c - t
    kpos = jnp.arange(nc)[:, None, None] * c - c + u[None]
    valid = (jnp.abs(rel) <= radius)[None] & (kpos >= 0) & (kpos < L)
    s = jnp.einsum('nbqhd,nbkhd->nbhqk', qp, kc, preferred_element_type=jnp.float32) * (dh ** -0.5)
    s = jnp.where(valid[None, :, None], s, NEG)
    m = jnp.max(s, axis=-1)
    e = jnp.exp(s - m[..., None])
    den = jnp.sum(e, axis=-1)
    o = jnp.einsum('nbhqk,nbkhd->nbqhd', e, vc.astype(jnp.float32))
    o = o / jnp.transpose(den, (0, 1, 3, 2))[..., None]
    o = o.reshape(N, Lp, H, dh)[:, :L]
    m = jnp.transpose(m, (0, 1, 3, 2)).reshape(N, Lp, H)[:, :L]
    den = jnp.transpose(den, (0, 1, 3, 2)).reshape(N, Lp, H)[:, :L]
    return o, m, den


def _dilated_mixture_attention(q, k, v):
    B, S, H, dh = q.shape
    outs, maxs, dens = [], [], []
    for window, dil in DIL_PATTERNS:
        radius = window // (2 * dil)
        L = S // dil

        def to_res(t):
            return t.reshape(B, L, dil, H, dh).transpose(0, 2, 1, 3, 4).reshape(B * dil, L, H, dh)

        o, m, den = _band_attention(to_res(q), to_res(k), to_res(v), radius)
        outs.append(o.reshape(B, dil, L, H, dh).transpose(0, 2, 1, 3, 4).reshape(B, S, H, dh))
        maxs.append(m.reshape(B, dil, L, H).transpose(0, 2, 1, 3).reshape(B, S, H))
        dens.append(den.reshape(B, dil, L, H).transpose(0, 2, 1, 3).reshape(B, S, H))
    m_all = jnp.stack(maxs)
    w = jnp.stack(dens) * jnp.exp(m_all - jnp.max(m_all, axis=0, keepdims=True))
    out = jnp.sum(w[..., None] * jnp.stack(outs), axis=0) / jnp.sum(w, axis=0)[..., None]
    return out.astype(q.dtype)


def _diff_attention(q, k, v, lam):
    B, S, H, _, dh = q.shape
    nb = S // Q_BLOCK
    qb = jnp.moveaxis(q.reshape(B, nb, Q_BLOCK, H, 2, dh), 1, 0)
    vf = v.astype(jnp.float32)

    def one_block(qblk):
        s = jnp.einsum('bqhmd,bkhmd->bhmqk', qblk, k, preferred_element_type=jnp.float32) * (dh ** -0.5)
        p = jax.nn.softmax(s, axis=-1)
        a = p[:, :, 0] - lam * p[:, :, 1]
        return jnp.einsum('bhqk,bkhe->bqhe', a, vf)

    o = lax.map(one_block, qb)
    return jnp.moveaxis(o, 0, 1).reshape(B, S, H, 2 * dh)


def _hier_moe(h, w_rg, w_re, w_gate, w_up, w_down):
    B, S, D = h.shape
    T = B * S
    xt = h.reshape(T, D)
    g_prob = jax.nn.softmax((xt @ w_rg).astype(jnp.float32), axis=-1)
    g_w, g_idx = lax.top_k(g_prob, 1)
    e_logits = (xt @ w_re).astype(jnp.float32).reshape(T, N_GROUPS, EXPERTS_PER_GROUP)
    e_sel = jnp.take_along_axis(e_logits, g_idx[:, :, None], axis=1)[:, 0]
    e_w, e_idx = lax.top_k(jax.nn.softmax(e_sel, axis=-1), TOP_K_INNER)
    e_w = e_w / jnp.sum(e_w, axis=-1, keepdims=True)
    gate = g_w * e_w
    expert = (g_idx * EXPERTS_PER_GROUP + e_idx).astype(jnp.int32)
    A = T * TOP_K_INNER
    flat_e = expert.reshape(A)
    flat_tok = jnp.repeat(jnp.arange(T, dtype=jnp.int32), TOP_K_INNER)
    flat_w = gate.reshape(A)
    order = jnp.argsort(flat_e, stable=True)
    se = flat_e[order]
    counts = jnp.bincount(flat_e, length=N_EXPERTS).astype(jnp.int32)
    starts = jnp.cumsum(counts) - counts
    padded = (counts + MOE_BLOCK - 1) // MOE_BLOCK * MOE_BLOCK
    pad_ends = jnp.cumsum(padded)
    pad_starts = pad_ends - padded
    dest = pad_starts[se] + jnp.arange(A, dtype=jnp.int32) - starts[se]
    R = A + N_EXPERTS * MOE_BLOCK
    n_blocks = R // MOE_BLOCK
    row_tok = jnp.full((R,), T, jnp.int32).at[dest].set(flat_tok[order])
    row_w = jnp.zeros((R,), jnp.float32).at[dest].set(flat_w[order])
    blk_start = jnp.arange(n_blocks, dtype=jnp.int32) * MOE_BLOCK
    blk_e = jnp.minimum(jnp.searchsorted(pad_ends, blk_start, side='right'), N_EXPERTS - 1).astype(jnp.int32)
    x_pad = jnp.concatenate([xt, jnp.zeros((1, D), xt.dtype)], axis=0)
    xb = x_pad[row_tok].reshape(n_blocks, MOE_BLOCK, D)

    def run_expert(args):
        xblk, e = args
        a = xblk @ w_gate[e]
        u = xblk @ w_up[e]
        return (jax.nn.silu(a) * u) @ w_down[e]

    yb = lax.map(run_expert, (xb, blk_e))
    y_rows = yb.reshape(R, D).astype(jnp.float32) * row_w[:, None]
    y = jax.ops.segment_sum(y_rows, row_tok, num_segments=T + 1)[:T]
    return y.reshape(B, S, D).astype(h.dtype)


def _trunk(x, norm1_g, w_in, lambda_q1, lambda_k1, lambda_q2, lambda_k2, subln_g, w_out,
           norm2_g, w_router_group, w_router_expert, w_gate, w_up, w_down, final_norm_g):
    B, S, _ = x.shape
    pos = jnp.arange(S, dtype=jnp.float32)
    splits = np.cumsum([WIDTH_A, WIDTH_A, WIDTH_A, QK_WIDTH_B, QK_WIDTH_B]).tolist()
    for l in range(DEPTH):
        h = _rmsnorm(x, norm1_g[l])
        proj = h @ w_in[l]
        qa, ka, va, qb, kb, vb = jnp.split(proj, splits, axis=-1)
        qa = _rope(qa.reshape(B, S, N_HEADS_A, HEAD_DIM), pos)
        ka = _rope(ka.reshape(B, S, N_HEADS_A, HEAD_DIM), pos)
        va = va.reshape(B, S, N_HEADS_A, HEAD_DIM)
        oa = _dilated_mixture_attention(qa, ka, va)
        qb = _rope(qb.reshape(B, S, 2 * N_HEADS_B, HEAD_DIM), pos).reshape(B, S, N_HEADS_B, 2, HEAD_DIM)
        kb = _rope(kb.reshape(B, S, 2 * N_HEADS_B, HEAD_DIM), pos).reshape(B, S, N_HEADS_B, 2, HEAD_DIM)
        vb = vb.reshape(B, S, N_HEADS_B, V_DIM_B)
        lam_init = 0.8 - 0.6 * math.exp(-0.3 * l)
        lam = (jnp.exp(jnp.sum(lambda_q1[l].astype(jnp.float32) * lambda_k1[l].astype(jnp.float32)))
               - jnp.exp(jnp.sum(lambda_q2[l].astype(jnp.float32) * lambda_k2[l].astype(jnp.float32)))
               + lam_init)
        ob = _diff_attention(qb, kb, vb, lam)
        ob = (_rmsnorm(ob, subln_g[l]).astype(jnp.float32) * (1.0 - lam_init)).astype(x.dtype)
        mix = jnp.concatenate([oa.reshape(B, S, WIDTH_A), ob.reshape(B, S, WIDTH_B)], axis=-1)
        x = x + mix @ w_out[l]
        h = _rmsnorm(x, norm2_g[l])
        x = x + _hier_moe(h, w_router_group[l], w_router_expert[l], w_gate[l], w_up[l], w_down[l])
    return _rmsnorm(x, final_norm_g)


def setup_inputs(seed: int = 0) -> dict:
    key = jax.random.key(seed)
    ks = jax.random.split(key, 20)
    f32 = jnp.float32
    nrm = lambda k, shape, scale: jax.random.normal(k, shape, f32) * scale
    return {
        "x_prompt": nrm(ks[0], (BATCH, SEQ, D_MODEL), 1.0),
        "x_sample": nrm(ks[1], (DEC_BATCH, DEC_SEQ, D_MODEL), 1.0),
        "norm1_g": 1.0 + nrm(ks[2], (DEPTH, D_MODEL), 0.02),
        "w_in": nrm(ks[3], (DEPTH, D_MODEL, IN_COLS), D_MODEL ** -0.5),
        "lambda_q1": nrm(ks[4], (DEPTH, HEAD_DIM), 0.1),
        "lambda_k1": nrm(ks[5], (DEPTH, HEAD_DIM), 0.1),
        "lambda_q2": nrm(ks[6], (DEPTH, HEAD_DIM), 0.1),
        "lambda_k2": nrm(ks[7], (DEPTH, HEAD_DIM), 0.1),
        "subln_g": 1.0 + nrm(ks[8], (DEPTH, V_DIM_B), 0.02),
        "w_out": nrm(ks[9], (DEPTH, MIX_WIDTH, D_MODEL), MIX_WIDTH ** -0.5),
        "norm2_g": 1.0 + nrm(ks[10], (DEPTH, D_MODEL), 0.02),
        "w_router_group": nrm(ks[11], (DEPTH, D_MODEL, N_GROUPS), D_MODEL ** -0.5),
        "w_router_expert": nrm(ks[12], (DEPTH, D_MODEL, N_EXPERTS), D_MODEL ** -0.5),
        "w_gate": nrm(ks[13], (DEPTH, N_EXPERTS, D_MODEL, D_FF_EXPERT), D_MODEL ** -0.5),
        "w_up": nrm(ks[14], (DEPTH, N_EXPERTS, D_MODEL, D_FF_EXPERT), D_MODEL ** -0.5),
        "w_down": nrm(ks[15], (DEPTH, N_EXPERTS, D_FF_EXPERT, D_MODEL), D_FF_EXPERT ** -0.5),
        "final_norm_g": 1.0 + nrm(ks[16], (D_MODEL,), 0.02),
    }


def reference(x_prompt, x_sample, norm1_g, w_in, lambda_q1, lambda_k1, lambda_q2, lambda_k2, subln_g,
              w_out, norm2_g, w_router_group, w_router_expert, w_gate, w_up, w_down, final_norm_g):
    y_prompt = _trunk(x_prompt, norm1_g, w_in, lambda_q1, lambda_k1, lambda_q2, lambda_k2, subln_g, w_out,
                      norm2_g, w_router_group, w_router_expert, w_gate, w_up, w_down, final_norm_g)
    y_sample = _trunk(x_sample, norm1_g, w_in, lambda_q1, lambda_k1, lambda_q2, lambda_k2, subln_g, w_out,
                      norm2_g, w_router_group, w_router_expert, w_gate, w_up, w_down, final_norm_g)
    return (y_prompt, y_sample)
```

```python
import functools
import math

import numpy as np
import jax
import jax.numpy as jnp
from jax import lax
from jax.experimental import pallas as pl
from jax.experimental.pallas import tpu as pltpu

HEAD_DIM = 64
N_HEADS_A = 8
WIDTH_A = N_HEADS_A * HEAD_DIM
DIL_PATTERNS = ((128, 1), (512, 4), (2048, 16))
N_HEADS_B = 4
QK_WIDTH_B = N_HEADS_B * 2 * HEAD_DIM
V_DIM_B = 2 * HEAD_DIM
WIDTH_B = N_HEADS_B * V_DIM_B
IN_COLS = 3 * WIDTH_A + 2 * QK_WIDTH_B + WIDTH_B
ROPE_THETA = 10000.0
EPS = 1e-6
N_GROUPS = 4
EXPERTS_PER_GROUP = 8
N_EXPERTS = N_GROUPS * EXPERTS_PER_GROUP
PAIRS_PER_GROUP = EXPERTS_PER_GROUP * (EXPERTS_PER_GROUP - 1) // 2
N_CLASSES = N_GROUPS * PAIRS_PER_GROUP
NEG = -1e30

LANES = 128
TOKEN_TILE = 512
Q_TILE_B = 256
BAND_BLOCK = 128
BAND_RADIUS = 64
MOE_BLOCK = 128
VMEM_LIMIT = 56 * 1024 * 1024

QK_SCALE = HEAD_DIM ** -0.5 * math.log2(math.e)

f32 = jnp.float32
bf16 = jnp.bfloat16


def _rope_tables(seq):
    half = HEAD_DIM // 2
    inv = ROPE_THETA ** (-np.arange(half, dtype=np.float64) / half)
    lane = np.arange(LANES)
    ang = np.arange(seq, dtype=np.float64)[:, None] * inv[lane % half][None, :]
    sign = np.where((lane % HEAD_DIM) < half, -1.0, 1.0)[None, :]
    return np.cos(ang).astype(np.float32), (np.sin(ang) * sign).astype(np.float32)


def _inproj_kernel(x_ref, g_ref, w_ref, cos_ref, sin_ref,
                   qa_ref, ka_ref, va_ref, qb_ref, kb_ref, vb_ref):
    x = x_ref[...]
    ms = jnp.mean(x * x, axis=-1, keepdims=True)
    h = (x * lax.rsqrt(ms + EPS) * g_ref[...]).astype(bf16)
    p = jnp.dot(h, w_ref[...], preferred_element_type=f32)
    cos = cos_ref[...]
    sin = sin_ref[...]
    lane = lax.broadcasted_iota(jnp.int32, cos.shape, 1)
    first_half = (lane & (HEAD_DIM - 1)) < (HEAD_DIM // 2)

    def rope(t):
        rot = jnp.where(first_half, pltpu.roll(t, LANES - HEAD_DIM // 2, 1),
                        pltpu.roll(t, HEAD_DIM // 2, 1))
        return t * cos + rot * sin

    plan = ((qa_ref, 0, True, QK_SCALE), (ka_ref, WIDTH_A, True, None),
            (va_ref, 2 * WIDTH_A, False, None),
            (qb_ref, 3 * WIDTH_A, True, QK_SCALE), (kb_ref, 3 * WIDTH_A + QK_WIDTH_B, True, None),
            (vb_ref, 3 * WIDTH_A + 2 * QK_WIDTH_B, False, None))
    for ref, off, rotary, scale in plan:
        for c in range(ref.shape[1] // LANES):
            t = p[:, off + c * LANES: off + (c + 1) * LANES]
            if rotary:
                t = rope(t)
            if scale is not None:
                t = t * scale
            ref[:, c * LANES:(c + 1) * LANES] = t.astype(bf16)


def _inproj(x2d, g, w, cos, sin, seq):
    T, D = x2d.shape
    tiles_per_seq = seq // TOKEN_TILE
    out = jax.ShapeDtypeStruct((T, WIDTH_A), bf16)
    row_spec = lambda width: pl.BlockSpec((TOKEN_TILE, width), lambda i: (i, 0))
    return pl.pallas_call(
        _inproj_kernel,
        grid=(T // TOKEN_TILE,),
        in_specs=[
            row_spec(D),
            pl.BlockSpec((1, D), lambda i: (0, 0)),
            pl.BlockSpec((D, IN_COLS), lambda i: (0, 0)),
            pl.BlockSpec((TOKEN_TILE, LANES), lambda i: (i % tiles_per_seq, 0)),
            pl.BlockSpec((TOKEN_TILE, LANES), lambda i: (i % tiles_per_seq, 0)),
        ],
        out_specs=[row_spec(WIDTH_A)] * 6,
        out_shape=[out] * 6,
        compiler_params=pltpu.CompilerParams(
            dimension_semantics=("arbitrary",), vmem_limit_bytes=VMEM_LIMIT),
    )(x2d, g, w, cos, sin)


def _band_block(q, k, v, offset):
    nq, nk = q.shape[0], k.shape[0]
    rel = (lax.broadcasted_iota(jnp.int32, (nq, nk), 0)
           - lax.broadcasted_iota(jnp.int32, (nq, nk), 1)) + offset
    valid = jnp.abs(rel) <= BAND_RADIUS
    lane = lax.broadcasted_iota(jnp.int32, (nq, LANES), 1)
    head0 = lane < HEAD_DIM
    parts = []
    for sel in (head0, jnp.logical_not(head0)):
        qm = jnp.where(sel, q, jnp.zeros_like(q))
        s = lax.dot_general(qm, k, (((1,), (1,)), ((), ())), preferred_element_type=f32)
        s = jnp.where(valid, s, NEG)
        m = jnp.max(s, axis=1, keepdims=True)
        e = jnp.exp2(s - m)
        l = jnp.sum(e, axis=1, keepdims=True)
        o = jnp.dot(e.astype(bf16), v, preferred_element_type=f32)
        parts.append((o, m, l))
    (o0, m0, l0), (o1, m1, l1) = parts
    return (jnp.where(head0, o0, o1),
            jnp.where(head0, jnp.broadcast_to(m0, (nq, LANES)), jnp.broadcast_to(m1, (nq, LANES))),
            jnp.where(head0, jnp.broadcast_to(l0, (nq, LANES)), jnp.broadcast_to(l1, (nq, LANES))))


def _mixer_a_kernel(q_ref, k_ref, v_ref, o_ref, qf, kf, vf, acc_s, m_s, l_s):
    seq = q_ref.shape[0]
    window = BAND_BLOCK + 2 * BAND_RADIUS
    qf[...] = q_ref[...].astype(f32)
    kf[...] = k_ref[...].astype(f32)
    vf[...] = v_ref[...].astype(f32)

    def dense_body(blk, carry):
        q0 = pl.multiple_of(blk * BAND_BLOCK, BAND_BLOCK)
        ws = pl.multiple_of(jnp.clip(q0 - BAND_RADIUS, 0, seq - window), BAND_RADIUS)
        o, m, l = _band_block(q_ref[pl.ds(q0, BAND_BLOCK), :], k_ref[pl.ds(ws, window), :],
                              v_ref[pl.ds(ws, window), :], q0 - ws)
        acc_s[pl.ds(q0, BAND_BLOCK), :] = o
        m_s[pl.ds(q0, BAND_BLOCK), :] = m
        l_s[pl.ds(q0, BAND_BLOCK), :] = l
        return carry

    lax.fori_loop(0, seq // BAND_BLOCK, dense_body, 0)

    for _, dil in DIL_PATTERNS[1:]:
        cls_len = seq // dil
        blocks = cls_len // BAND_BLOCK

        def dilated_body(it, carry, dil=dil, cls_len=cls_len, blocks=blocks):
            r = it // blocks
            blk = it % blocks
            q0 = blk * BAND_BLOCK
            ws = jnp.clip(q0 - BAND_RADIUS, 0, cls_len - window)
            q_rows = pl.ds(r + dil * q0, BAND_BLOCK, stride=dil)
            k_rows = pl.ds(r + dil * ws, window, stride=dil)
            o, m, l = _band_block(qf[q_rows, :].astype(bf16), kf[k_rows, :].astype(bf16),
                                  vf[k_rows, :].astype(bf16), q0 - ws)
            m_old = m_s[q_rows, :]
            m_new = jnp.maximum(m_old, m)
            a_old = jnp.exp2(m_old - m_new)
            a_blk = jnp.exp2(m - m_new)
            acc_s[q_rows, :] = acc_s[q_rows, :] * a_old + o * a_blk
            l_s[q_rows, :] = l_s[q_rows, :] * a_old + l * a_blk
            m_s[q_rows, :] = m_new
            return carry

        lax.fori_loop(0, dil * blocks, dilated_body, 0)

    o_ref[...] = (acc_s[...] / l_s[...]).astype(o_ref.dtype)


def _mixer_a(q, k, v):
    nb, seq, width = q.shape
    spec = pl.BlockSpec((None, seq, LANES), lambda b, p: (b, 0, p))
    return pl.pallas_call(
        _mixer_a_kernel,
        grid=(nb, width // LANES),
        in_specs=[spec, spec, spec],
        out_specs=spec,
        out_shape=jax.ShapeDtypeStruct((nb, seq, width), bf16),
        scratch_shapes=[pltpu.VMEM((seq, LANES), f32)] * 6,
        compiler_params=pltpu.CompilerParams(
            dimension_semantics=("arbitrary", "arbitrary"), vmem_limit_bytes=VMEM_LIMIT),
    )(q, k, v)


def _mixer_b_kernel(lam_ref, q_ref, k_ref, v_ref, g_ref, o_ref, *, out_scale):
    q = q_ref[...]
    k = k_ref[...]
    lane = lax.broadcasted_iota(jnp.int32, q.shape, 1)
    first = lane < HEAD_DIM
    zero = jnp.zeros_like(q)
    nt = (((1,), (1,)), ((), ()))
    s0 = lax.dot_general(jnp.where(first, q, zero), k, nt, preferred_element_type=f32)
    s1 = lax.dot_general(jnp.where(first, zero, q), k, nt, preferred_element_type=f32)
    e0 = jnp.exp2(s0 - jnp.max(s0, axis=1, keepdims=True))
    e1 = jnp.exp2(s1 - jnp.max(s1, axis=1, keepdims=True))
    c0 = 1.0 / jnp.sum(e0, axis=1, keepdims=True)
    c1 = lam_ref[0] / jnp.sum(e1, axis=1, keepdims=True)
    a = (e0 * c0 - e1 * c1).astype(bf16)
    o = jnp.dot(a, v_ref[...], preferred_element_type=f32)
    ms = jnp.mean(o * o, axis=-1, keepdims=True)
    o_ref[...] = (o * lax.rsqrt(ms + EPS) * g_ref[...] * out_scale).astype(o_ref.dtype)


def _mixer_b(lam, q, k, v, g, out_scale):
    nb, seq, width = q.shape
    heads = width // LANES
    q_spec = pl.BlockSpec((None, Q_TILE_B, LANES), lambda b, h, i, lam: (b, i, h))
    kv_spec = pl.BlockSpec((None, seq, LANES), lambda b, h, i, lam: (b, 0, h))
    return pl.pallas_call(
        functools.partial(_mixer_b_kernel, out_scale=out_scale),
        grid_spec=pltpu.PrefetchScalarGridSpec(
            num_scalar_prefetch=1,
            grid=(nb, heads, seq // Q_TILE_B),
            in_specs=[q_spec, kv_spec, kv_spec,
                      pl.BlockSpec((1, LANES), lambda b, h, i, lam: (0, 0))],
            out_specs=q_spec,
        ),
        out_shape=jax.ShapeDtypeStruct((nb, seq, width), bf16),
        compiler_params=pltpu.CompilerParams(
            dimension_semantics=("arbitrary", "arbitrary", "arbitrary"),
            vmem_limit_bytes=VMEM_LIMIT),
    )(lam, q, k, v, g)


def _outproj_kernel(oa_ref, ob_ref, x_ref, wa_ref, wb_ref, g_ref, wr_hi_ref, wr_lo_ref,
                    x1_ref, info_ref):
    x1 = (x_ref[...]
          + jnp.dot(oa_ref[...], wa_ref[...], preferred_element_type=f32)
          + jnp.dot(ob_ref[...], wb_ref[...], preferred_element_type=f32))
    x1_ref[...] = x1
    ms = jnp.mean(x1 * x1, axis=-1, keepdims=True)
    h = x1 * lax.rsqrt(ms + EPS) * g_ref[...]
    h_hi = h.astype(bf16)
    h_lo = (h - h_hi.astype(f32)).astype(bf16)
    logits = (jnp.dot(h_hi, wr_hi_ref[...], preferred_element_type=f32)
              + jnp.dot(h_hi, wr_lo_ref[...], preferred_element_type=f32)
              + jnp.dot(h_lo, wr_hi_ref[...], preferred_element_type=f32))

    lane = lax.broadcasted_iota(jnp.int32, logits.shape, 1)
    lane_f = lane.astype(f32)

    def first_argmax(vals):
        top = jnp.max(vals, axis=1, keepdims=True)
        first = jnp.min(jnp.where(vals == top, lane_f, float(LANES)), axis=1, keepdims=True)
        return top, first.astype(jnp.int32)

    is_group = lane < N_GROUPS
    g_top, g_idx = first_argmax(jnp.where(is_group, logits, NEG))
    g_w = 1.0 / jnp.sum(jnp.where(is_group, jnp.exp(logits - g_top), 0.0), axis=1, keepdims=True)

    group_first = N_GROUPS + g_idx * EXPERTS_PER_GROUP
    in_group = (lane >= group_first) & (lane < group_first + EXPERTS_PER_GROUP)
    e_logits = jnp.where(in_group, logits, NEG)
    top1, i1 = first_argmax(e_logits)
    top2, i2 = first_argmax(jnp.where(lane == i1, NEG, e_logits))
    r = jnp.exp(top2 - top1)
    w1 = g_w / (1.0 + r)
    w2 = g_w * r / (1.0 + r)
    k1 = i1 - group_first
    k2 = i2 - group_first
    lo = jnp.minimum(k1, k2)
    hi = jnp.maximum(k1, k2)
    pair = jnp.right_shift(lo * (2 * EXPERTS_PER_GROUP - 1 - lo), 1) + (hi - lo - 1)
    cls = (g_idx * PAIRS_PER_GROUP + pair).astype(f32)
    w_lo = jnp.where(k1 < k2, w1, w2)
    w_hi = jnp.where(k1 < k2, w2, w1)
    info_ref[...] = jnp.where(lane == 0, cls, jnp.where(lane == 1, w_lo,
                                                       jnp.where(lane == 2, w_hi, 0.0)))


def _outproj(oa, ob, x2d, wa, wb, g, wr_hi, wr_lo):
    T, D = x2d.shape
    row_spec = lambda width: pl.BlockSpec((TOKEN_TILE, width), lambda i: (i, 0))
    full = lambda a: pl.BlockSpec(a.shape, lambda i: (0, 0))
    return pl.pallas_call(
        _outproj_kernel,
        grid=(T // TOKEN_TILE,),
        in_specs=[row_spec(WIDTH_A), row_spec(WIDTH_B), row_spec(D), full(wa), full(wb), full(g),
                  full(wr_hi), full(wr_lo)],
        out_specs=[row_spec(D), row_spec(LANES)],
        out_shape=[jax.ShapeDtypeStruct((T, D), f32), jax.ShapeDtypeStruct((T, LANES), f32)],
        compiler_params=pltpu.CompilerParams(
            dimension_semantics=("arbitrary",), vmem_limit_bytes=VMEM_LIMIT),
    )(oa, ob, x2d, wa, wb, g, wr_hi, wr_lo)


def _pair_tables():
    lo, hi = [], []
    for g in range(N_GROUPS):
        for a in range(EXPERTS_PER_GROUP):
            for b in range(a + 1, EXPERTS_PER_GROUP):
                lo.append(g * EXPERTS_PER_GROUP + a)
                hi.append(g * EXPERTS_PER_GROUP + b)
    return np.asarray(lo, np.int32), np.asarray(hi, np.int32)


def _dispatch_plan(cls, n_rows):
    T = cls.shape[0]
    n_blocks = n_rows // MOE_BLOCK
    counts = jnp.bincount(cls, length=N_CLASSES).astype(jnp.int32)
    starts = jnp.cumsum(counts) - counts
    padded = (counts + MOE_BLOCK - 1) // MOE_BLOCK * MOE_BLOCK
    pad_ends = jnp.cumsum(padded)
    pad_starts = pad_ends - padded
    order = jnp.argsort(cls, stable=True).astype(jnp.int32)
    sorted_cls = cls[order]
    dest = pad_starts[sorted_cls] + jnp.arange(T, dtype=jnp.int32) - starts[sorted_cls]
    row_tok = jnp.zeros((n_rows,), jnp.int32).at[dest].set(order)
    blk_start = jnp.arange(n_blocks, dtype=jnp.int32) * MOE_BLOCK
    blk_cls = jnp.minimum(jnp.searchsorted(pad_ends, blk_start, side='right'),
                          N_CLASSES - 1).astype(jnp.int32)
    blk_valid = jnp.clip(counts[blk_cls] - (blk_start - pad_starts[blk_cls]), 0, MOE_BLOCK)
    blk_valid = jnp.where(blk_start < pad_ends[-1], blk_valid, 0).astype(jnp.int32)
    lo_tab, hi_tab = _pair_tables()
    return row_tok, jnp.asarray(lo_tab)[blk_cls], jnp.asarray(hi_tab)[blk_cls], blk_valid


def _moe_kernel(ea_ref, eb_ref, nv_ref, tok_ref, gate_ref, x_hbm, g2_ref, gf_ref,
                wgu_a_ref, wd_a_ref, wgu_b_ref, wd_b_ref, o_hbm, xbuf, obuf, gsem, ssem,
                *, final_norm):
    del ea_ref, eb_ref
    i = pl.program_id(0)
    n_valid = nv_ref[i]

    def gather_copy(r):
        return pltpu.make_async_copy(x_hbm.at[pl.ds(tok_ref[0, r], 1)], xbuf.at[pl.ds(r, 1)], gsem)

    def scatter_copy(r):
        return pltpu.make_async_copy(obuf.at[pl.ds(r, 1)], o_hbm.at[pl.ds(tok_ref[0, r], 1)], ssem)

    @pl.when(n_valid > 0)
    def _():
        def start_gather(r, c):
            gather_copy(r).start()
            return c
        lax.fori_loop(0, MOE_BLOCK, start_gather, 0)

        def wait_gather(r, c):
            gather_copy(r).wait()
            return c
        lax.fori_loop(0, MOE_BLOCK, wait_gather, 0)

        x = xbuf[...]
        ms = jnp.mean(x * x, axis=-1, keepdims=True)
        h = (x * lax.rsqrt(ms + EPS) * g2_ref[...]).astype(bf16)
        gates = gate_ref[...]
        d_ff = wd_a_ref.shape[0]
        y = x
        for col, (wgu_ref, wd_ref) in enumerate(((wgu_a_ref, wd_a_ref), (wgu_b_ref, wd_b_ref))):
            gu = jnp.dot(h, wgu_ref[...], preferred_element_type=f32)
            a = gu[:, :d_ff]
            act = (a * jax.nn.sigmoid(a) * gu[:, d_ff:]).astype(bf16)
            y = y + gates[:, col:col + 1] * jnp.dot(act, wd_ref[...], preferred_element_type=f32)
        if final_norm:
            ms = jnp.mean(y * y, axis=-1, keepdims=True)
            y = y * lax.rsqrt(ms + EPS) * gf_ref[...]
        obuf[...] = y

        def start_scatter(r, c):
            scatter_copy(r).start()
            return c
        lax.fori_loop(0, n_valid, start_scatter, 0)

        def wait_scatter(r, c):
            scatter_copy(r).wait()
            return c
        lax.fori_loop(0, n_valid, wait_scatter, 0)


def _moe(x1, info, g2, gf, wgu, wd, final_norm):
    T, D = x1.shape
    n_rows = T + N_CLASSES * MOE_BLOCK
    n_blocks = n_rows // MOE_BLOCK
    cls = info[:, 0].astype(jnp.int32)
    row_tok, blk_ea, blk_eb, blk_valid = _dispatch_plan(cls, n_rows)
    row_gates = info[:, 1:3][row_tok]
    d_ff = wd.shape[1]
    idx = lambda f: (lambda i, ea, eb, nv: f(i, ea, eb))
    return pl.pallas_call(
        functools.partial(_moe_kernel, final_norm=final_norm),
        grid_spec=pltpu.PrefetchScalarGridSpec(
            num_scalar_prefetch=3,
            grid=(n_blocks,),
            in_specs=[
                pl.BlockSpec((None, 1, MOE_BLOCK), idx(lambda i, ea, eb: (i, 0, 0)),
                             memory_space=pltpu.SMEM),
                pl.BlockSpec((MOE_BLOCK, 2), idx(lambda i, ea, eb: (i, 0))),
                pl.BlockSpec(memory_space=pl.ANY),
                pl.BlockSpec((1, D), idx(lambda i, ea, eb: (0, 0))),
                pl.BlockSpec((1, D), idx(lambda i, ea, eb: (0, 0))),
                pl.BlockSpec((None, D, 2 * d_ff), idx(lambda i, ea, eb: (ea[i], 0, 0))),
                pl.BlockSpec((None, d_ff, D), idx(lambda i, ea, eb: (ea[i], 0, 0))),
                pl.BlockSpec((None, D, 2 * d_ff), idx(lambda i, ea, eb: (eb[i], 0, 0))),
                pl.BlockSpec((None, d_ff, D), idx(lambda i, ea, eb: (eb[i], 0, 0))),
            ],
            out_specs=pl.BlockSpec(memory_space=pl.ANY),
            scratch_shapes=[pltpu.VMEM((MOE_BLOCK, D), f32), pltpu.VMEM((MOE_BLOCK, D), f32),
                            pltpu.SemaphoreType.DMA, pltpu.SemaphoreType.DMA],
        ),
        out_shape=jax.ShapeDtypeStruct((T, D), f32),
        compiler_params=pltpu.CompilerParams(
            dimension_semantics=("arbitrary",), vmem_limit_bytes=VMEM_LIMIT),
    )(blk_ea, blk_eb, blk_valid, row_tok.reshape(n_blocks, 1, MOE_BLOCK), row_gates, x1, g2, gf,
      wgu, wd, wgu, wd)


def kernel(x_prompt, x_sample, norm1_g, w_in, lambda_q1, lambda_k1, lambda_q2, lambda_k2, subln_g,
           w_out, norm2_g, w_router_group, w_router_expert, w_gate, w_up, w_down, final_norm_g):
    n_prompt = x_prompt.shape[0]
    x = jnp.concatenate([x_prompt, x_sample], axis=0)
    nb, seq, D = x.shape
    depth = w_in.shape[0]
    x2d = x.reshape(nb * seq, D)
    cos_np, sin_np = _rope_tables(seq)
    cos, sin = jnp.asarray(cos_np), jnp.asarray(sin_np)
    row = lambda v: v.reshape(1, -1).astype(f32)

    for l in range(depth):
        qa, ka, va, qb, kb, vb = _inproj(x2d, row(norm1_g[l]), w_in[l].astype(bf16), cos, sin, seq)
        to_seq = lambda t: t.reshape(nb, seq, t.shape[-1])
        oa = _mixer_a(to_seq(qa), to_seq(ka), to_seq(va))
        lam_init = 0.8 - 0.6 * math.exp(-0.3 * l)
        lam = (jnp.exp(jnp.sum(lambda_q1[l].astype(f32) * lambda_k1[l].astype(f32)))
               - jnp.exp(jnp.sum(lambda_q2[l].astype(f32) * lambda_k2[l].astype(f32)))
               + lam_init).reshape(1)
        ob = _mixer_b(lam, to_seq(qb), to_seq(kb), to_seq(vb), row(subln_g[l]), 1.0 - lam_init)

        w_router = jnp.concatenate([w_router_group[l], w_router_expert[l]], axis=1).astype(f32)
        w_router = jnp.pad(w_router, ((0, 0), (0, LANES - w_router.shape[1])))
        wr_hi = w_router.astype(bf16)
        wr_lo = (w_router - wr_hi.astype(f32)).astype(bf16)
        wo = w_out[l].astype(bf16)
        x1, info = _outproj(oa.reshape(nb * seq, WIDTH_A), ob.reshape(nb * seq, WIDTH_B), x2d,
                            wo[:WIDTH_A], wo[WIDTH_A:], row(norm2_g[l]), wr_hi, wr_lo)

        wgu = jnp.concatenate([w_gate[l], w_up[l]], axis=-1).astype(bf16)
        x2d = _moe(x1, info, row(norm2_g[l]), row(final_norm_g), wgu, w_down[l].astype(bf16),
                   final_norm=(l == depth - 1))

    y = x2d.reshape(nb, seq, D)
    return y[:n_prompt], y[n_prompt:]
```

```python
import functools
import math

import numpy as np
import jax
import jax.numpy as jnp
from jax import lax
from jax.experimental import pallas as pl
from jax.experimental.pallas import tpu as pltpu

HEAD_DIM = 64
N_HEADS_A = 8
WIDTH_A = N_HEADS_A * HEAD_DIM
DIL_PATTERNS = ((128, 1), (512, 4), (2048, 16))
N_HEADS_B = 4
QK_WIDTH_B = N_HEADS_B * 2 * HEAD_DIM
V_DIM_B = 2 * HEAD_DIM
WIDTH_B = N_HEADS_B * V_DIM_B
IN_COLS = 3 * WIDTH_A + 2 * QK_WIDTH_B + WIDTH_B
ROPE_THETA = 10000.0
EPS = 1e-6
N_GROUPS = 4
EXPERTS_PER_GROUP = 8
N_EXPERTS = N_GROUPS * EXPERTS_PER_GROUP
PAIRS_PER_GROUP = EXPERTS_PER_GROUP * (EXPERTS_PER_GROUP - 1) // 2
N_CLASSES = N_GROUPS * PAIRS_PER_GROUP
NEG = -1e30

LANES = 128
SUBLANES = 8
TOKEN_TILE = 512
Q_TILE_B = 256
BAND_BLOCK = 128
BAND_RADIUS = 64
BAND_WINDOW = BAND_BLOCK + 2 * BAND_RADIUS
BAND_UNROLL = 4
MOE_BLOCK = 256
VMEM_LIMIT = 56 * 1024 * 1024

QK_SCALE = HEAD_DIM ** -0.5 * math.log2(math.e)

f32 = jnp.float32
bf16 = jnp.bfloat16


def _rope_tables(seq):
    half = HEAD_DIM // 2
    inv = ROPE_THETA ** (-np.arange(half, dtype=np.float64) / half)
    lane = np.arange(LANES)
    ang = np.arange(seq, dtype=np.float64)[:, None] * inv[lane % half][None, :]
    sign = np.where((lane % HEAD_DIM) < half, -1.0, 1.0)[None, :]
    return np.cos(ang).astype(np.float32), (np.sin(ang) * sign).astype(np.float32)


def _inproj_kernel(x_ref, g_ref, w_ref, cos_ref, sin_ref,
                   qa_ref, ka_ref, va_ref, qb_ref, kb_ref, vb_ref):
    x = x_ref[...]
    ms = jnp.mean(x * x, axis=-1, keepdims=True)
    h = (x * lax.rsqrt(ms + EPS) * g_ref[...]).astype(bf16)
    p = jnp.dot(h, w_ref[...], preferred_element_type=f32)
    cos = cos_ref[...]
    sin = sin_ref[...]
    lane = lax.broadcasted_iota(jnp.int32, cos.shape, 1)
    first_half = (lane & (HEAD_DIM - 1)) < (HEAD_DIM // 2)

    def rope(t):
        rot = jnp.where(first_half, pltpu.roll(t, LANES - HEAD_DIM // 2, 1),
                        pltpu.roll(t, HEAD_DIM // 2, 1))
        return t * cos + rot * sin

    plan = ((qa_ref, 0, True, QK_SCALE), (ka_ref, WIDTH_A, True, None),
            (va_ref, 2 * WIDTH_A, False, None),
            (qb_ref, 3 * WIDTH_A, True, QK_SCALE), (kb_ref, 3 * WIDTH_A + QK_WIDTH_B, True, None),
            (vb_ref, 3 * WIDTH_A + 2 * QK_WIDTH_B, False, None))
    for ref, off, rotary, scale in plan:
        for c in range(ref.shape[1] // LANES):
            t = p[:, off + c * LANES: off + (c + 1) * LANES]
            if rotary:
                t = rope(t)
            if scale is not None:
                t = t * scale
            ref[:, c * LANES:(c + 1) * LANES] = t.astype(bf16)


def _inproj(x2d, g, w, cos, sin, seq):
    T, D = x2d.shape
    tiles_per_seq = seq // TOKEN_TILE
    out = jax.ShapeDtypeStruct((T, WIDTH_A), bf16)
    row_spec = lambda width: pl.BlockSpec((TOKEN_TILE, width), lambda i: (i, 0))
    return pl.pallas_call(
        _inproj_kernel,
        grid=(T // TOKEN_TILE,),
        in_specs=[
            row_spec(D),
            pl.BlockSpec((1, D), lambda i: (0, 0)),
            pl.BlockSpec((D, IN_COLS), lambda i: (0, 0)),
            pl.BlockSpec((TOKEN_TILE, LANES), lambda i: (i % tiles_per_seq, 0)),
            pl.BlockSpec((TOKEN_TILE, LANES), lambda i: (i % tiles_per_seq, 0)),
        ],
        out_specs=[row_spec(WIDTH_A)] * 6,
        out_shape=[out] * 6,
        compiler_params=pltpu.CompilerParams(
            dimension_semantics=("arbitrary",), vmem_limit_bytes=VMEM_LIMIT),
    )(x2d, g, w, cos, sin)


def _band_block(q, k, v, bias):
    nq = q.shape[0]
    lane = lax.broadcasted_iota(jnp.int32, (nq, LANES), 1)
    head0 = lane < HEAD_DIM
    parts = []
    for sel in (head0, jnp.logical_not(head0)):
        qm = jnp.where(sel, q, jnp.zeros_like(q))
        s = lax.dot_general(qm, k, (((1,), (1,)), ((), ())), preferred_element_type=f32) + bias
        m = jnp.max(s, axis=1, keepdims=True)
        e = jnp.exp2(s - m)
        l = jnp.sum(e, axis=1, keepdims=True)
        o = jnp.dot(e.astype(bf16), v, preferred_element_type=f32)
        parts.append((o, m, l))
    (o0, m0, l0), (o1, m1, l1) = parts
    return (jnp.where(head0, o0, o1),
            jnp.where(head0, jnp.broadcast_to(m0, (nq, LANES)), jnp.broadcast_to(m1, (nq, LANES))),
            jnp.where(head0, jnp.broadcast_to(l0, (nq, LANES)), jnp.broadcast_to(l1, (nq, LANES))))


def _mixer_a_kernel(q_ref, k_ref, v_ref, o_ref, qf, kf, vf, qd, kd, vd, oc, mc, lc,
                    acc_s, m_s, l_s, bias_s):
    seq = q_ref.shape[0]
    n_blocks = seq // BAND_BLOCK

    rel = (lax.broadcasted_iota(jnp.int32, (BAND_BLOCK, BAND_WINDOW), 0)
           - lax.broadcasted_iota(jnp.int32, (BAND_BLOCK, BAND_WINDOW), 1))
    for j in range(3):
        bias_s[j] = jnp.where(jnp.abs(rel + j * BAND_RADIUS) <= BAND_RADIUS, 0.0, NEG)

    staged = False
    for _, dil in DIL_PATTERNS:
        cls_len = seq // dil
        blocks_per_cls = cls_len // BAND_BLOCK
        if dil == 1:
            src, dst = (q_ref, k_ref, v_ref), (acc_s, m_s, l_s)
        else:
            if not staged:
                qf[...] = q_ref[...].astype(f32)
                kf[...] = k_ref[...].astype(f32)
                vf[...] = v_ref[...].astype(f32)
                staged = True
            for r in range(dil):
                for major, natural in ((qd, qf), (kd, kf), (vd, vf)):
                    major[pl.ds(r * cls_len, cls_len), :] = (
                        natural[pl.ds(r, cls_len, stride=dil), :].astype(bf16))
            src, dst = (qd, kd, vd), (oc, mc, lc)

        def body(it, carry, src=src, dst=dst, cls_len=cls_len, blocks_per_cls=blocks_per_cls):
            for u in range(BAND_UNROLL):
                b = it * BAND_UNROLL + u
                blk = b % blocks_per_cls
                base = (b // blocks_per_cls) * cls_len
                q0 = blk * BAND_BLOCK
                ws = jnp.clip(q0 - BAND_RADIUS, 0, cls_len - BAND_WINDOW)
                q_rows = pl.ds(pl.multiple_of(base + q0, BAND_BLOCK), BAND_BLOCK)
                k_rows = pl.ds(pl.multiple_of(base + ws, BAND_RADIUS), BAND_WINDOW)
                o, m, l = _band_block(src[0][q_rows, :], src[1][k_rows, :], src[2][k_rows, :],
                                      bias_s[(q0 - ws) // BAND_RADIUS])
                dst[0][q_rows, :] = o
                dst[1][q_rows, :] = m
                dst[2][q_rows, :] = l
            return carry

        lax.fori_loop(0, n_blocks // BAND_UNROLL, body, 0)

        if dil > 1:
            for r in range(dil):
                nat = pl.ds(r, cls_len, stride=dil)
                cm = pl.ds(r * cls_len, cls_len)
                m_old = m_s[nat, :]
                m_pat = mc[cm, :]
                m_new = jnp.maximum(m_old, m_pat)
                a_old = jnp.exp2(m_old - m_new)
                a_pat = jnp.exp2(m_pat - m_new)
                acc_s[nat, :] = acc_s[nat, :] * a_old + oc[cm, :] * a_pat
                l_s[nat, :] = l_s[nat, :] * a_old + lc[cm, :] * a_pat
                m_s[nat, :] = m_new

    o_ref[...] = (acc_s[...] / l_s[...]).astype(o_ref.dtype)


def _mixer_a(q, k, v):
    nb, seq, width = q.shape
    spec = pl.BlockSpec((None, seq, LANES), lambda b, p: (b, 0, p))
    tile_f32 = pltpu.VMEM((seq, LANES), f32)
    tile_bf16 = pltpu.VMEM((seq, LANES), bf16)
    return pl.pallas_call(
        _mixer_a_kernel,
        grid=(nb, width // LANES),
        in_specs=[spec, spec, spec],
        out_specs=spec,
        out_shape=jax.ShapeDtypeStruct((nb, seq, width), bf16),
        scratch_shapes=[tile_f32] * 3 + [tile_bf16] * 3 + [tile_f32] * 6
        + [pltpu.VMEM((3, BAND_BLOCK, BAND_WINDOW), f32)],
        compiler_params=pltpu.CompilerParams(
            dimension_semantics=("arbitrary", "arbitrary"), vmem_limit_bytes=VMEM_LIMIT),
    )(q, k, v)


def _mixer_b_kernel(lam_ref, q_ref, k_ref, v_ref, g_ref, o_ref, *, out_scale):
    q = q_ref[...]
    k = k_ref[...]
    lane = lax.broadcasted_iota(jnp.int32, q.shape, 1)
    first = lane < HEAD_DIM
    zero = jnp.zeros_like(q)
    nt = (((1,), (1,)), ((), ()))
    s0 = lax.dot_general(jnp.where(first, q, zero), k, nt, preferred_element_type=f32)
    s1 = lax.dot_general(jnp.where(first, zero, q), k, nt, preferred_element_type=f32)
    e0 = jnp.exp2(s0 - jnp.max(s0, axis=1, keepdims=True))
    e1 = jnp.exp2(s1 - jnp.max(s1, axis=1, keepdims=True))
    c0 = 1.0 / jnp.sum(e0, axis=1, keepdims=True)
    c1 = lam_ref[0] / jnp.sum(e1, axis=1, keepdims=True)
    a = (e0 * c0 - e1 * c1).astype(bf16)
    o = jnp.dot(a, v_ref[...], preferred_element_type=f32)
    ms = jnp.mean(o * o, axis=-1, keepdims=True)
    o_ref[...] = (o * lax.rsqrt(ms + EPS) * g_ref[...] * out_scale).astype(o_ref.dtype)


def _mixer_b(lam, q, k, v, g, out_scale):
    nb, seq, width = q.shape
    heads = width // LANES
    q_spec = pl.BlockSpec((None, Q_TILE_B, LANES), lambda b, h, i, lam: (b, i, h))
    kv_spec = pl.BlockSpec((None, seq, LANES), lambda b, h, i, lam: (b, 0, h))
    return pl.pallas_call(
        functools.partial(_mixer_b_kernel, out_scale=out_scale),
        grid_spec=pltpu.PrefetchScalarGridSpec(
            num_scalar_prefetch=1,
            grid=(nb, heads, seq // Q_TILE_B),
            in_specs=[q_spec, kv_spec, kv_spec,
                      pl.BlockSpec((1, LANES), lambda b, h, i, lam: (0, 0))],
            out_specs=q_spec,
        ),
        out_shape=jax.ShapeDtypeStruct((nb, seq, width), bf16),
        compiler_params=pltpu.CompilerParams(
            dimension_semantics=("arbitrary", "arbitrary", "arbitrary"),
            vmem_limit_bytes=VMEM_LIMIT),
    )(lam, q, k, v, g)


INFO_CLASS, INFO_GATE_LO, INFO_GATE_HI = 0, 1, 2


def _outproj_kernel(oa_ref, ob_ref, x_ref, wa_ref, wb_ref, g_ref, wr_hi_ref, wr_lo_ref,
                    xi_ref, info_t_ref, counts_ref):
    D = x_ref.shape[1]
    x1 = (x_ref[...]
          + jnp.dot(oa_ref[...], wa_ref[...], preferred_element_type=f32)
          + jnp.dot(ob_ref[...], wb_ref[...], preferred_element_type=f32))
    xi_ref[:, :D] = x1
    ms = jnp.mean(x1 * x1, axis=-1, keepdims=True)
    h = x1 * lax.rsqrt(ms + EPS) * g_ref[...]
    h_hi = h.astype(bf16)
    h_lo = (h - h_hi.astype(f32)).astype(bf16)
    logits = (jnp.dot(h_hi, wr_hi_ref[...], preferred_element_type=f32)
              + jnp.dot(h_hi, wr_lo_ref[...], preferred_element_type=f32)
              + jnp.dot(h_lo, wr_hi_ref[...], preferred_element_type=f32))

    lane = lax.broadcasted_iota(jnp.int32, logits.shape, 1)
    lane_f = lane.astype(f32)

    def first_argmax(vals):
        top = jnp.max(vals, axis=1, keepdims=True)
        first = jnp.min(jnp.where(vals == top, lane_f, float(LANES)), axis=1, keepdims=True)
        return top, first.astype(jnp.int32)

    is_group = lane < N_GROUPS
    g_top, g_idx = first_argmax(jnp.where(is_group, logits, NEG))
    g_w = 1.0 / jnp.sum(jnp.where(is_group, jnp.exp(logits - g_top), 0.0), axis=1, keepdims=True)

    group_first = N_GROUPS + g_idx * EXPERTS_PER_GROUP
    in_group = (lane >= group_first) & (lane < group_first + EXPERTS_PER_GROUP)
    e_logits = jnp.where(in_group, logits, NEG)
    top1, i1 = first_argmax(e_logits)
    top2, i2 = first_argmax(jnp.where(lane == i1, NEG, e_logits))
    r = jnp.exp(top2 - top1)
    w1 = g_w / (1.0 + r)
    w2 = g_w * r / (1.0 + r)
    k1 = i1 - group_first
    k2 = i2 - group_first
    lo = jnp.minimum(k1, k2)
    hi = jnp.maximum(k1, k2)
    pair = jnp.right_shift(lo * (2 * EXPERTS_PER_GROUP - 1 - lo), 1) + (hi - lo - 1)
    cls = g_idx * PAIRS_PER_GROUP + pair
    w_lo = jnp.where(k1 < k2, w1, w2)
    w_hi = jnp.where(k1 < k2, w2, w1)
    info = jnp.where(lane == INFO_CLASS, cls.astype(f32),
                     jnp.where(lane == INFO_GATE_LO, w_lo,
                               jnp.where(lane == INFO_GATE_HI, w_hi, 0.0)))
    xi_ref[:, D:] = info
    info_t_ref[...] = jnp.transpose(info)[:SUBLANES, :]

    @pl.when(pl.program_id(0) == 0)
    def _():
        counts_ref[...] = jnp.zeros_like(counts_ref)

    counts_ref[...] += jnp.sum(jnp.where(lane == cls, 1.0, 0.0), axis=0, keepdims=True)


def _outproj(oa, ob, x2d, wa, wb, g, wr_hi, wr_lo):
    T, D = x2d.shape
    row_spec = lambda width: pl.BlockSpec((TOKEN_TILE, width), lambda i: (i, 0))
    full = lambda a: pl.BlockSpec(a.shape, lambda i: (0, 0))
    return pl.pallas_call(
        _outproj_kernel,
        grid=(T // TOKEN_TILE,),
        in_specs=[row_spec(WIDTH_A), row_spec(WIDTH_B), row_spec(D), full(wa), full(wb), full(g),
                  full(wr_hi), full(wr_lo)],
        out_specs=[row_spec(D + LANES),
                   pl.BlockSpec((SUBLANES, TOKEN_TILE), lambda i: (0, i)),
                   pl.BlockSpec((1, LANES), lambda i: (0, 0))],
        out_shape=[jax.ShapeDtypeStruct((T, D + LANES), f32),
                   jax.ShapeDtypeStruct((SUBLANES, T), f32),
                   jax.ShapeDtypeStruct((1, LANES), f32)],
        compiler_params=pltpu.CompilerParams(
            dimension_semantics=("arbitrary",), vmem_limit_bytes=VMEM_LIMIT),
    )(oa, ob, x2d, wa, wb, g, wr_hi, wr_lo)


def _pair_tables():
    lo, hi = [], []
    for g in range(N_GROUPS):
        for a in range(EXPERTS_PER_GROUP):
            for b in range(a + 1, EXPERTS_PER_GROUP):
                lo.append(g * EXPERTS_PER_GROUP + a)
                hi.append(g * EXPERTS_PER_GROUP + b)
    return np.asarray(lo, np.int32), np.asarray(hi, np.int32)


def _block_plan(counts, n_blocks):
    counts = counts.astype(jnp.int32)
    padded = (counts + MOE_BLOCK - 1) // MOE_BLOCK * MOE_BLOCK
    pad_ends = jnp.cumsum(padded)
    pad_starts = pad_ends - padded
    blk_start = jnp.arange(n_blocks, dtype=jnp.int32) * MOE_BLOCK
    blk_cls = jnp.minimum(jnp.sum(pad_ends[None, :] <= blk_start[:, None], axis=1),
                          N_CLASSES - 1).astype(jnp.int32)
    onehot = (blk_cls[:, None] == jnp.arange(N_CLASSES, dtype=jnp.int32)[None, :]).astype(jnp.int32)
    used = jnp.sum(onehot * (counts - (blk_start[:, None] - pad_starts[None, :])), axis=1)
    blk_valid = jnp.where(blk_start < pad_ends[-1], jnp.clip(used, 0, MOE_BLOCK), 0).astype(jnp.int32)
    lo_tab, hi_tab = _pair_tables()
    blk_ea = jnp.sum(onehot * jnp.asarray(lo_tab)[None, :], axis=1).astype(jnp.int32)
    blk_eb = jnp.sum(onehot * jnp.asarray(hi_tab)[None, :], axis=1).astype(jnp.int32)
    return pad_starts, blk_ea, blk_eb, blk_valid


def _positions_kernel(info_t_ref, start_ref, pos_ref, base_s):
    n = info_t_ref.shape[1]

    @pl.when(pl.program_id(0) == 0)
    def _():
        base_s[...] = start_ref[...]

    cls = info_t_ref[INFO_CLASS:INFO_CLASS + 1, :].astype(jnp.int32)
    onehot = lax.broadcasted_iota(jnp.int32, (LANES, n), 0) == cls
    earlier = (lax.broadcasted_iota(jnp.int32, (n, n), 0)
               < lax.broadcasted_iota(jnp.int32, (n, n), 1))
    onehot_f = jnp.where(onehot, 1.0, 0.0)
    rank = jnp.dot(onehot_f.astype(bf16), jnp.where(earlier, 1.0, 0.0).astype(bf16),
                   preferred_element_type=f32)
    base = base_s[...]
    pos = jnp.sum(jnp.where(onehot, rank + base, 0.0), axis=0, keepdims=True)
    pos_ref[...] = pos.astype(jnp.int32)
    base_s[...] = base + jnp.sum(onehot_f, axis=1, keepdims=True)


def _positions(info_t, pad_starts):
    T = info_t.shape[1]
    start_col = jnp.zeros((LANES, 1), f32).at[:N_CLASSES, 0].set(pad_starts.astype(f32))
    return pl.pallas_call(
        _positions_kernel,
        grid=(T // TOKEN_TILE,),
        in_specs=[pl.BlockSpec((SUBLANES, TOKEN_TILE), lambda i: (0, i)),
                  pl.BlockSpec((LANES, 1), lambda i: (0, 0))],
        out_specs=pl.BlockSpec((1, TOKEN_TILE), lambda i: (0, i)),
        out_shape=jax.ShapeDtypeStruct((1, T), jnp.int32),
        scratch_shapes=[pltpu.VMEM((LANES, 1), f32)],
        compiler_params=pltpu.CompilerParams(
            dimension_semantics=("arbitrary",), vmem_limit_bytes=VMEM_LIMIT),
    )(info_t, start_col)


def _permute_kernel(pos_ref, src_hbm, *rest, scatter):
    dst_hbm, sem = rest[-2:]
    n = pos_ref.shape[1]
    first = pl.program_id(0) * n

    def row_copy(r):
        p = pl.ds(pos_ref[0, r], 1)
        t = pl.ds(first + r, 1)
        if scatter:
            return pltpu.make_async_copy(src_hbm.at[t], dst_hbm.at[p], sem)
        return pltpu.make_async_copy(src_hbm.at[p], dst_hbm.at[t], sem)

    def start(r, c):
        row_copy(r).start()
        return c
    lax.fori_loop(0, n, start, 0)

    def wait(r, c):
        row_copy(r).wait()
        return c
    lax.fori_loop(0, n, wait, 0)


def _permute_rows(pos3, src, n_out_rows, scatter):
    n_tiles, _, tile = pos3.shape
    out_shape = jax.ShapeDtypeStruct((n_out_rows, src.shape[1]), src.dtype)
    operands = (pos3, src) + ((jnp.zeros(out_shape.shape, out_shape.dtype),) if scatter else ())
    return pl.pallas_call(
        functools.partial(_permute_kernel, scatter=scatter),
        grid=(n_tiles,),
        in_specs=[pl.BlockSpec((None, 1, tile), lambda i: (i, 0, 0), memory_space=pltpu.SMEM)]
        + [pl.BlockSpec(memory_space=pl.ANY)] * (len(operands) - 1),
        out_specs=pl.BlockSpec(memory_space=pl.ANY),
        out_shape=out_shape,
        scratch_shapes=[pltpu.SemaphoreType.DMA],
        input_output_aliases={2: 0} if scatter else {},
        compiler_params=pltpu.CompilerParams(dimension_semantics=("arbitrary",)),
    )(*operands)


def _experts_kernel(ea_ref, eb_ref, nv_ref, xs_ref, g2_ref, gf_ref,
                    wgu_a_ref, wd_a_ref, wgu_b_ref, wd_b_ref, ys_ref, *, final_norm):
    del ea_ref, eb_ref
    D = ys_ref.shape[1]
    d_ff = wd_a_ref.shape[0]
    active = nv_ref[pl.program_id(0)] > 0

    @pl.when(active)
    def _():
        x = xs_ref[:, :D]
        info = xs_ref[:, D:]
        ms = jnp.mean(x * x, axis=-1, keepdims=True)
        h = (x * lax.rsqrt(ms + EPS) * g2_ref[...]).astype(bf16)
        y = x
        for lane, wgu_ref, wd_ref in ((INFO_GATE_LO, wgu_a_ref, wd_a_ref),
                                      (INFO_GATE_HI, wgu_b_ref, wd_b_ref)):
            gu = jnp.dot(h, wgu_ref[...], preferred_element_type=f32)
            a = gu[:, :d_ff]
            act = (a * jax.nn.sigmoid(a) * gu[:, d_ff:]).astype(bf16)
            y = y + info[:, lane:lane + 1] * jnp.dot(act, wd_ref[...], preferred_element_type=f32)
        if final_norm:
            ms = jnp.mean(y * y, axis=-1, keepdims=True)
            y = y * lax.rsqrt(ms + EPS) * gf_ref[...]
        ys_ref[...] = y

    @pl.when(jnp.logical_not(active))
    def _():
        ys_ref[...] = jnp.zeros_like(ys_ref)


def _experts(xs, blk_ea, blk_eb, blk_valid, g2, gf, wgu, wd, final_norm):
    n_rows, width = xs.shape
    D = width - LANES
    d_ff = wd.shape[1]
    idx = lambda f: (lambda i, ea, eb, nv: f(i, ea, eb))
    return pl.pallas_call(
        functools.partial(_experts_kernel, final_norm=final_norm),
        grid_spec=pltpu.PrefetchScalarGridSpec(
            num_scalar_prefetch=3,
            grid=(n_rows // MOE_BLOCK,),
            in_specs=[
                pl.BlockSpec((MOE_BLOCK, width), idx(lambda i, ea, eb: (i, 0))),
                pl.BlockSpec((1, D), idx(lambda i, ea, eb: (0, 0))),
                pl.BlockSpec((1, D), idx(lambda i, ea, eb: (0, 0))),
                pl.BlockSpec((None, D, 2 * d_ff), idx(lambda i, ea, eb: (ea[i], 0, 0))),
                pl.BlockSpec((None, d_ff, D), idx(lambda i, ea, eb: (ea[i], 0, 0))),
                pl.BlockSpec((None, D, 2 * d_ff), idx(lambda i, ea, eb: (eb[i], 0, 0))),
                pl.BlockSpec((None, d_ff, D), idx(lambda i, ea, eb: (eb[i], 0, 0))),
            ],
            out_specs=pl.BlockSpec((MOE_BLOCK, D), idx(lambda i, ea, eb: (i, 0))),
        ),
        out_shape=jax.ShapeDtypeStruct((n_rows, D), f32),
        compiler_params=pltpu.CompilerParams(
            dimension_semantics=("arbitrary",), vmem_limit_bytes=VMEM_LIMIT),
    )(blk_ea, blk_eb, blk_valid, xs, g2, gf, wgu, wd, wgu, wd)


def _moe(xi, info_t, counts, g2, gf, wgu, wd, final_norm):
    T, width = xi.shape
    D = width - LANES
    n_rows = T + N_CLASSES * MOE_BLOCK
    pad_starts, blk_ea, blk_eb, blk_valid = _block_plan(counts[0, :N_CLASSES], n_rows // MOE_BLOCK)
    pos3 = _positions(info_t, pad_starts).reshape(T // TOKEN_TILE, 1, TOKEN_TILE)
    xs = _permute_rows(pos3, xi, n_rows, scatter=True)
    ys = _experts(xs, blk_ea, blk_eb, blk_valid, g2, gf, wgu, wd, final_norm)
    return _permute_rows(pos3, ys, T, scatter=False)


def kernel(x_prompt, x_sample, norm1_g, w_in, lambda_q1, lambda_k1, lambda_q2, lambda_k2, subln_g,
           w_out, norm2_g, w_router_group, w_router_expert, w_gate, w_up, w_down, final_norm_g):
    n_prompt = x_prompt.shape[0]
    x = jnp.concatenate([x_prompt, x_sample], axis=0)
    nb, seq, D = x.shape
    depth = w_in.shape[0]
    x2d = x.reshape(nb * seq, D)
    cos_np, sin_np = _rope_tables(seq)
    cos, sin = jnp.asarray(cos_np), jnp.asarray(sin_np)
    row = lambda v: v.reshape(1, -1).astype(f32)

    for l in range(depth):
        qa, ka, va, qb, kb, vb = _inproj(x2d, row(norm1_g[l]), w_in[l].astype(bf16), cos, sin, seq)
        to_seq = lambda t: t.reshape(nb, seq, t.shape[-1])
        oa = _mixer_a(to_seq(qa), to_seq(ka), to_seq(va))
        lam_init = 0.8 - 0.6 * math.exp(-0.3 * l)
        lam = (jnp.exp(jnp.sum(lambda_q1[l].astype(f32) * lambda_k1[l].astype(f32)))
               - jnp.exp(jnp.sum(lambda_q2[l].astype(f32) * lambda_k2[l].astype(f32)))
               + lam_init).reshape(1)
        ob = _mixer_b(lam, to_seq(qb), to_seq(kb), to_seq(vb), row(subln_g[l]), 1.0 - lam_init)

        w_router = jnp.concatenate([w_router_group[l], w_router_expert[l]], axis=1).astype(f32)
        w_router = jnp.pad(w_router, ((0, 0), (0, LANES - w_router.shape[1])))
        wr_hi = w_router.astype(bf16)
        wr_lo = (w_router - wr_hi.astype(f32)).astype(bf16)
        wo = w_out[l].astype(bf16)
        xi, info_t, counts = _outproj(oa.reshape(nb * seq, WIDTH_A), ob.reshape(nb * seq, WIDTH_B),
                                      x2d, wo[:WIDTH_A], wo[WIDTH_A:], row(norm2_g[l]), wr_hi, wr_lo)

        wgu = jnp.concatenate([w_gate[l], w_up[l]], axis=-1).astype(bf16)
        x2d = _moe(xi, info_t, counts, row(norm2_g[l]), row(final_norm_g), wgu,
                   w_down[l].astype(bf16), final_norm=(l == depth - 1))

    y = x2d.reshape(nb, seq, D)
    return y[:n_prompt], y[n_prompt:]
```

```python
import functools
import math

import numpy as np
import jax
import jax.numpy as jnp
from jax import lax
from jax.experimental import pallas as pl
from jax.experimental.pallas import tpu as pltpu

HEAD_DIM = 64
N_HEADS_A = 8
WIDTH_A = N_HEADS_A * HEAD_DIM
DIL_PATTERNS = ((128, 1), (512, 4), (2048, 16))
N_HEADS_B = 4
QK_WIDTH_B = N_HEADS_B * 2 * HEAD_DIM
V_DIM_B = 2 * HEAD_DIM
WIDTH_B = N_HEADS_B * V_DIM_B
IN_COLS = 3 * WIDTH_A + 2 * QK_WIDTH_B + WIDTH_B
ROPE_THETA = 10000.0
EPS = 1e-6
N_GROUPS = 4
EXPERTS_PER_GROUP = 8
N_EXPERTS = N_GROUPS * EXPERTS_PER_GROUP
PAIRS_PER_GROUP = EXPERTS_PER_GROUP * (EXPERTS_PER_GROUP - 1) // 2
N_CLASSES = N_GROUPS * PAIRS_PER_GROUP
NEG = -1e30

LANES = 128
SUBLANES = 8
TOKEN_TILE = 512
Q_TILE_B = 256
BAND_BLOCK = 128
BAND_RADIUS = 64
BAND_WINDOW = BAND_BLOCK + 2 * BAND_RADIUS
BAND_UNROLL = 4
MOE_BLOCK = 256
DMA_UNROLL = 8
VMEM_LIMIT = 56 * 1024 * 1024

QK_SCALE = HEAD_DIM ** -0.5 * math.log2(math.e)

f32 = jnp.float32
bf16 = jnp.bfloat16


def _rope_tables(seq):
    half = HEAD_DIM // 2
    inv = ROPE_THETA ** (-np.arange(half, dtype=np.float64) / half)
    lane = np.arange(LANES)
    ang = np.arange(seq, dtype=np.float64)[:, None] * inv[lane % half][None, :]
    sign = np.where((lane % HEAD_DIM) < half, -1.0, 1.0)[None, :]
    return np.cos(ang).astype(np.float32), (np.sin(ang) * sign).astype(np.float32)


def _inproj_kernel(x_ref, g_ref, w_ref, cos_ref, sin_ref,
                   qa_ref, ka_ref, va_ref, qb_ref, kb_ref, vb_ref):
    x = x_ref[...]
    ms = jnp.mean(x * x, axis=-1, keepdims=True)
    h = (x * lax.rsqrt(ms + EPS) * g_ref[...]).astype(bf16)
    p = jnp.dot(h, w_ref[...], preferred_element_type=f32)
    cos = cos_ref[...]
    sin = sin_ref[...]
    lane = lax.broadcasted_iota(jnp.int32, cos.shape, 1)
    first_half = (lane & (HEAD_DIM - 1)) < (HEAD_DIM // 2)

    def rope(t):
        rot = jnp.where(first_half, pltpu.roll(t, LANES - HEAD_DIM // 2, 1),
                        pltpu.roll(t, HEAD_DIM // 2, 1))
        return t * cos + rot * sin

    plan = ((qa_ref, 0, True, QK_SCALE), (ka_ref, WIDTH_A, True, None),
            (va_ref, 2 * WIDTH_A, False, None),
            (qb_ref, 3 * WIDTH_A, True, QK_SCALE), (kb_ref, 3 * WIDTH_A + QK_WIDTH_B, True, None),
            (vb_ref, 3 * WIDTH_A + 2 * QK_WIDTH_B, False, None))
    for ref, off, rotary, scale in plan:
        for c in range(ref.shape[1] // LANES):
            t = p[:, off + c * LANES: off + (c + 1) * LANES]
            if rotary:
                t = rope(t)
            if scale is not None:
                t = t * scale
            ref[:, c * LANES:(c + 1) * LANES] = t.astype(bf16)


def _inproj(x2d, g, w, cos, sin, seq):
    T, D = x2d.shape
    tiles_per_seq = seq // TOKEN_TILE
    out = jax.ShapeDtypeStruct((T, WIDTH_A), bf16)
    row_spec = lambda width: pl.BlockSpec((TOKEN_TILE, width), lambda i: (i, 0))
    return pl.pallas_call(
        _inproj_kernel,
        grid=(T // TOKEN_TILE,),
        in_specs=[
            row_spec(D),
            pl.BlockSpec((1, D), lambda i: (0, 0)),
            pl.BlockSpec((D, IN_COLS), lambda i: (0, 0)),
            pl.BlockSpec((TOKEN_TILE, LANES), lambda i: (i % tiles_per_seq, 0)),
            pl.BlockSpec((TOKEN_TILE, LANES), lambda i: (i % tiles_per_seq, 0)),
        ],
        out_specs=[row_spec(WIDTH_A)] * 6,
        out_shape=[out] * 6,
        compiler_params=pltpu.CompilerParams(
            dimension_semantics=("arbitrary",), vmem_limit_bytes=VMEM_LIMIT),
    )(x2d, g, w, cos, sin)


def _band_block(q, k, v, bias):
    nq = q.shape[0]
    lane = lax.broadcasted_iota(jnp.int32, (nq, LANES), 1)
    head0 = lane < HEAD_DIM
    parts = []
    for sel in (head0, jnp.logical_not(head0)):
        qm = jnp.where(sel, q, jnp.zeros_like(q))
        s = lax.dot_general(qm, k, (((1,), (1,)), ((), ())), preferred_element_type=f32) + bias
        m = jnp.max(s, axis=1, keepdims=True)
        e = jnp.exp2(s - m)
        l = jnp.sum(e, axis=1, keepdims=True)
        o = jnp.dot(e.astype(bf16), v, preferred_element_type=f32)
        parts.append((o, m, l))
    (o0, m0, l0), (o1, m1, l1) = parts
    return (jnp.where(head0, o0, o1),
            jnp.where(head0, jnp.broadcast_to(m0, (nq, LANES)), jnp.broadcast_to(m1, (nq, LANES))),
            jnp.where(head0, jnp.broadcast_to(l0, (nq, LANES)), jnp.broadcast_to(l1, (nq, LANES))))


def _mixer_a_kernel(q_ref, k_ref, v_ref, o_ref, qf, kf, vf, qd, kd, vd, oc, mc, lc,
                    acc_s, m_s, l_s, bias_s):
    seq = q_ref.shape[0]
    n_blocks = seq // BAND_BLOCK

    rel = (lax.broadcasted_iota(jnp.int32, (BAND_BLOCK, BAND_WINDOW), 0)
           - lax.broadcasted_iota(jnp.int32, (BAND_BLOCK, BAND_WINDOW), 1))
    for j in range(3):
        bias_s[j] = jnp.where(jnp.abs(rel + j * BAND_RADIUS) <= BAND_RADIUS, 0.0, NEG)

    staged = False
    for _, dil in DIL_PATTERNS:
        cls_len = seq // dil
        blocks_per_cls = cls_len // BAND_BLOCK
        if dil == 1:
            src, dst = (q_ref, k_ref, v_ref), (acc_s, m_s, l_s)
        else:
            if not staged:
                qf[...] = q_ref[...].astype(f32)
                kf[...] = k_ref[...].astype(f32)
                vf[...] = v_ref[...].astype(f32)
                staged = True
            for r in range(dil):
                for major, natural in ((qd, qf), (kd, kf), (vd, vf)):
                    major[pl.ds(r * cls_len, cls_len), :] = (
                        natural[pl.ds(r, cls_len, stride=dil), :].astype(bf16))
            src, dst = (qd, kd, vd), (oc, mc, lc)

        def body(it, carry, src=src, dst=dst, cls_len=cls_len, blocks_per_cls=blocks_per_cls):
            for u in range(BAND_UNROLL):
                b = it * BAND_UNROLL + u
                blk = b % blocks_per_cls
                base = (b // blocks_per_cls) * cls_len
                q0 = blk * BAND_BLOCK
                ws = jnp.clip(q0 - BAND_RADIUS, 0, cls_len - BAND_WINDOW)
                q_rows = pl.ds(pl.multiple_of(base + q0, BAND_BLOCK), BAND_BLOCK)
                k_rows = pl.ds(pl.multiple_of(base + ws, BAND_RADIUS), BAND_WINDOW)
                o, m, l = _band_block(src[0][q_rows, :], src[1][k_rows, :], src[2][k_rows, :],
                                      bias_s[(q0 - ws) // BAND_RADIUS])
                dst[0][q_rows, :] = o
                dst[1][q_rows, :] = m
                dst[2][q_rows, :] = l
            return carry

        lax.fori_loop(0, n_blocks // BAND_UNROLL, body, 0)

        if dil > 1:
            for r in range(dil):
                nat = pl.ds(r, cls_len, stride=dil)
                cm = pl.ds(r * cls_len, cls_len)
                m_old = m_s[nat, :]
                m_pat = mc[cm, :]
                m_new = jnp.maximum(m_old, m_pat)
                a_old = jnp.exp2(m_old - m_new)
                a_pat = jnp.exp2(m_pat - m_new)
                acc_s[nat, :] = acc_s[nat, :] * a_old + oc[cm, :] * a_pat
                l_s[nat, :] = l_s[nat, :] * a_old + lc[cm, :] * a_pat
                m_s[nat, :] = m_new

    o_ref[...] = (acc_s[...] / l_s[...]).astype(o_ref.dtype)


def _mixer_a(q, k, v):
    nb, seq, width = q.shape
    spec = pl.BlockSpec((None, seq, LANES), lambda b, p: (b, 0, p))
    tile_f32 = pltpu.VMEM((seq, LANES), f32)
    tile_bf16 = pltpu.VMEM((seq, LANES), bf16)
    return pl.pallas_call(
        _mixer_a_kernel,
        grid=(nb, width // LANES),
        in_specs=[spec, spec, spec],
        out_specs=spec,
        out_shape=jax.ShapeDtypeStruct((nb, seq, width), bf16),
        scratch_shapes=[tile_f32] * 3 + [tile_bf16] * 3 + [tile_f32] * 6
        + [pltpu.VMEM((3, BAND_BLOCK, BAND_WINDOW), f32)],
        compiler_params=pltpu.CompilerParams(
            dimension_semantics=("arbitrary", "arbitrary"), vmem_limit_bytes=VMEM_LIMIT),
    )(q, k, v)


def _mixer_b_kernel(lam_ref, q_ref, k_ref, v_ref, g_ref, o_ref, *, out_scale):
    q = q_ref[...]
    k = k_ref[...]
    lane = lax.broadcasted_iota(jnp.int32, q.shape, 1)
    first = lane < HEAD_DIM
    zero = jnp.zeros_like(q)
    nt = (((1,), (1,)), ((), ()))
    s0 = lax.dot_general(jnp.where(first, q, zero), k, nt, preferred_element_type=f32)
    s1 = lax.dot_general(jnp.where(first, zero, q), k, nt, preferred_element_type=f32)
    e0 = jnp.exp2(s0 - jnp.max(s0, axis=1, keepdims=True))
    e1 = jnp.exp2(s1 - jnp.max(s1, axis=1, keepdims=True))
    c0 = 1.0 / jnp.sum(e0, axis=1, keepdims=True)
    c1 = lam_ref[0] / jnp.sum(e1, axis=1, keepdims=True)
    a = (e0 * c0 - e1 * c1).astype(bf16)
    o = jnp.dot(a, v_ref[...], preferred_element_type=f32)
    ms = jnp.mean(o * o, axis=-1, keepdims=True)
    o_ref[...] = (o * lax.rsqrt(ms + EPS) * g_ref[...] * out_scale).astype(o_ref.dtype)


def _mixer_b(lam, q, k, v, g, out_scale):
    nb, seq, width = q.shape
    heads = width // LANES
    q_spec = pl.BlockSpec((None, Q_TILE_B, LANES), lambda b, h, i, lam: (b, i, h))
    kv_spec = pl.BlockSpec((None, seq, LANES), lambda b, h, i, lam: (b, 0, h))
    return pl.pallas_call(
        functools.partial(_mixer_b_kernel, out_scale=out_scale),
        grid_spec=pltpu.PrefetchScalarGridSpec(
            num_scalar_prefetch=1,
            grid=(nb, heads, seq // Q_TILE_B),
            in_specs=[q_spec, kv_spec, kv_spec,
                      pl.BlockSpec((1, LANES), lambda b, h, i, lam: (0, 0))],
            out_specs=q_spec,
        ),
        out_shape=jax.ShapeDtypeStruct((nb, seq, width), bf16),
        compiler_params=pltpu.CompilerParams(
            dimension_semantics=("arbitrary", "arbitrary", "arbitrary"),
            vmem_limit_bytes=VMEM_LIMIT),
    )(lam, q, k, v, g)


def _outproj_kernel(oa_ref, ob_ref, x_ref, wa_ref, wb_ref, g_ref, wr_hi_ref, wr_lo_ref,
                    x1_ref, cls_t_ref, counts_ref):
    x1 = (x_ref[...]
          + jnp.dot(oa_ref[...], wa_ref[...], preferred_element_type=f32)
          + jnp.dot(ob_ref[...], wb_ref[...], preferred_element_type=f32))
    x1_ref[...] = x1
    ms = jnp.mean(x1 * x1, axis=-1, keepdims=True)
    h = x1 * lax.rsqrt(ms + EPS) * g_ref[...]
    h_hi = h.astype(bf16)
    h_lo = (h - h_hi.astype(f32)).astype(bf16)
    logits = (jnp.dot(h_hi, wr_hi_ref[...], preferred_element_type=f32)
              + jnp.dot(h_hi, wr_lo_ref[...], preferred_element_type=f32)
              + jnp.dot(h_lo, wr_hi_ref[...], preferred_element_type=f32))

    lane = lax.broadcasted_iota(jnp.int32, logits.shape, 1)
    lane_f = lane.astype(f32)

    def first_argmax(vals):
        top = jnp.max(vals, axis=1, keepdims=True)
        first = jnp.min(jnp.where(vals == top, lane_f, float(LANES)), axis=1, keepdims=True)
        return first.astype(jnp.int32)

    g_idx = first_argmax(jnp.where(lane < N_GROUPS, logits, NEG))
    group_first = N_GROUPS + g_idx * EXPERTS_PER_GROUP
    in_group = (lane >= group_first) & (lane < group_first + EXPERTS_PER_GROUP)
    e_logits = jnp.where(in_group, logits, NEG)
    i1 = first_argmax(e_logits)
    i2 = first_argmax(jnp.where(lane == i1, NEG, e_logits))
    lo = jnp.minimum(i1, i2) - group_first
    hi = jnp.maximum(i1, i2) - group_first
    pair = jnp.right_shift(lo * (2 * EXPERTS_PER_GROUP - 1 - lo), 1) + (hi - lo - 1)
    cls = g_idx * PAIRS_PER_GROUP + pair
    cls_lanes = jnp.broadcast_to(cls.astype(f32), logits.shape)
    cls_t_ref[...] = jnp.transpose(cls_lanes)[:SUBLANES, :]

    @pl.when(pl.program_id(0) == 0)
    def _():
        counts_ref[...] = jnp.zeros_like(counts_ref)

    counts_ref[...] += jnp.sum(jnp.where(lane == cls, 1.0, 0.0), axis=0, keepdims=True)


def _outproj(oa, ob, x2d, wa, wb, g, wr_hi, wr_lo):
    T, D = x2d.shape
    row_spec = lambda width: pl.BlockSpec((TOKEN_TILE, width), lambda i: (i, 0))
    full = lambda a: pl.BlockSpec(a.shape, lambda i: (0, 0))
    return pl.pallas_call(
        _outproj_kernel,
        grid=(T // TOKEN_TILE,),
        in_specs=[row_spec(WIDTH_A), row_spec(WIDTH_B), row_spec(D), full(wa), full(wb), full(g),
                  full(wr_hi), full(wr_lo)],
        out_specs=[row_spec(D),
                   pl.BlockSpec((SUBLANES, TOKEN_TILE), lambda i: (0, i)),
                   pl.BlockSpec((1, LANES), lambda i: (0, 0))],
        out_shape=[jax.ShapeDtypeStruct((T, D), f32),
                   jax.ShapeDtypeStruct((SUBLANES, T), f32),
                   jax.ShapeDtypeStruct((1, LANES), f32)],
        compiler_params=pltpu.CompilerParams(
            dimension_semantics=("arbitrary",), vmem_limit_bytes=VMEM_LIMIT),
    )(oa, ob, x2d, wa, wb, g, wr_hi, wr_lo)


def _pair_tables():
    lo, hi = [], []
    for g in range(N_GROUPS):
        for a in range(EXPERTS_PER_GROUP):
            for b in range(a + 1, EXPERTS_PER_GROUP):
                lo.append(g * EXPERTS_PER_GROUP + a)
                hi.append(g * EXPERTS_PER_GROUP + b)
    return np.asarray(lo, np.int32), np.asarray(hi, np.int32)


def _block_plan(counts, n_blocks):
    counts = counts.astype(jnp.int32)
    padded = (counts + MOE_BLOCK - 1) // MOE_BLOCK * MOE_BLOCK
    pad_ends = jnp.cumsum(padded)
    pad_starts = pad_ends - padded
    blk_start = jnp.arange(n_blocks, dtype=jnp.int32) * MOE_BLOCK
    blk_cls = jnp.minimum(jnp.sum(pad_ends[None, :] <= blk_start[:, None], axis=1),
                          N_CLASSES - 1).astype(jnp.int32)
    onehot = (blk_cls[:, None] == jnp.arange(N_CLASSES, dtype=jnp.int32)[None, :]).astype(jnp.int32)
    used = jnp.sum(onehot * (counts - (blk_start[:, None] - pad_starts[None, :])), axis=1)
    blk_valid = jnp.where(blk_start < pad_ends[-1], jnp.clip(used, 0, MOE_BLOCK), 0).astype(jnp.int32)
    lo_tab, hi_tab = _pair_tables()
    blk_ea = jnp.sum(onehot * jnp.asarray(lo_tab)[None, :], axis=1).astype(jnp.int32)
    blk_eb = jnp.sum(onehot * jnp.asarray(hi_tab)[None, :], axis=1).astype(jnp.int32)
    return pad_starts, blk_ea, blk_eb, blk_valid


def _positions_kernel(cls_t_ref, start_ref, pos_ref, base_s):
    n = cls_t_ref.shape[1]

    @pl.when(pl.program_id(0) == 0)
    def _():
        base_s[...] = start_ref[...]

    cls = cls_t_ref[0:1, :].astype(jnp.int32)
    onehot = lax.broadcasted_iota(jnp.int32, (LANES, n), 0) == cls
    earlier = (lax.broadcasted_iota(jnp.int32, (n, n), 0)
               < lax.broadcasted_iota(jnp.int32, (n, n), 1))
    onehot_f = jnp.where(onehot, 1.0, 0.0)
    rank = jnp.dot(onehot_f.astype(bf16), jnp.where(earlier, 1.0, 0.0).astype(bf16),
                   preferred_element_type=f32)
    base = base_s[...]
    pos = jnp.sum(jnp.where(onehot, rank + base, 0.0), axis=0, keepdims=True)
    pos_ref[...] = pos.astype(jnp.int32)
    base_s[...] = base + jnp.sum(onehot_f, axis=1, keepdims=True)


def _positions(cls_t, pad_starts):
    T = cls_t.shape[1]
    start_col = jnp.zeros((LANES, 1), f32).at[:N_CLASSES, 0].set(pad_starts.astype(f32))
    return pl.pallas_call(
        _positions_kernel,
        grid=(T // TOKEN_TILE,),
        in_specs=[pl.BlockSpec((SUBLANES, TOKEN_TILE), lambda i: (0, i)),
                  pl.BlockSpec((LANES, 1), lambda i: (0, 0))],
        out_specs=pl.BlockSpec((1, TOKEN_TILE), lambda i: (0, i)),
        out_shape=jax.ShapeDtypeStruct((1, T), jnp.int32),
        scratch_shapes=[pltpu.VMEM((LANES, 1), f32)],
        compiler_params=pltpu.CompilerParams(
            dimension_semantics=("arbitrary",), vmem_limit_bytes=VMEM_LIMIT),
    )(cls_t, start_col)


def _token_copy_loops(make_copy, n):
    def start(r, c):
        make_copy(r).start()
        return c
    lax.fori_loop(0, n, start, 0, unroll=DMA_UNROLL)

    def wait(r, c):
        make_copy(r).wait()
        return c
    lax.fori_loop(0, n, wait, 0, unroll=DMA_UNROLL)


def _dispatch_kernel(pos_ref, x_ref, init_hbm, xs_hbm, stage, sem):
    del init_hbm
    n, D = x_ref.shape
    chunks = D // LANES
    for j in range(chunks):
        stage[pl.ds(j, n, stride=chunks), :] = x_ref[:, j * LANES:(j + 1) * LANES]

    def copy(r):
        src = stage.at[pl.ds(pl.multiple_of(r * chunks, chunks), chunks)]
        dst = xs_hbm.at[pl.ds(pl.multiple_of(pos_ref[0, r] * chunks, chunks), chunks)]
        return pltpu.make_async_copy(src, dst, sem)

    _token_copy_loops(copy, n)


def _dispatch(pos3, x1, n_rows):
    n_tiles, _, tile = pos3.shape
    D = x1.shape[1]
    chunks = D // LANES
    shape = jax.ShapeDtypeStruct((n_rows * chunks, LANES), f32)
    return pl.pallas_call(
        _dispatch_kernel,
        grid=(n_tiles,),
        in_specs=[pl.BlockSpec((None, 1, tile), lambda i: (i, 0, 0), memory_space=pltpu.SMEM),
                  pl.BlockSpec((tile, D), lambda i: (i, 0)),
                  pl.BlockSpec(memory_space=pl.ANY)],
        out_specs=pl.BlockSpec(memory_space=pl.ANY),
        out_shape=shape,
        scratch_shapes=[pltpu.VMEM((tile * chunks, LANES), f32), pltpu.SemaphoreType.DMA],
        input_output_aliases={2: 0},
        compiler_params=pltpu.CompilerParams(
            dimension_semantics=("arbitrary",), vmem_limit_bytes=VMEM_LIMIT),
    )(pos3, x1, jnp.zeros(shape.shape, f32))


def _combine_kernel(pos_ref, ys_hbm, o_ref, stage, sem):
    n, D = o_ref.shape
    chunks = D // LANES

    def copy(r):
        src = ys_hbm.at[pl.ds(pl.multiple_of(pos_ref[0, r] * chunks, chunks), chunks)]
        dst = stage.at[pl.ds(pl.multiple_of(r * chunks, chunks), chunks)]
        return pltpu.make_async_copy(src, dst, sem)

    _token_copy_loops(copy, n)
    for j in range(chunks):
        o_ref[:, j * LANES:(j + 1) * LANES] = stage[pl.ds(j, n, stride=chunks), :]


def _combine(pos3, ys, D):
    n_tiles, _, tile = pos3.shape
    chunks = D // LANES
    return pl.pallas_call(
        _combine_kernel,
        grid=(n_tiles,),
        in_specs=[pl.BlockSpec((None, 1, tile), lambda i: (i, 0, 0), memory_space=pltpu.SMEM),
                  pl.BlockSpec(memory_space=pl.ANY)],
        out_specs=pl.BlockSpec((tile, D), lambda i: (i, 0)),
        out_shape=jax.ShapeDtypeStruct((n_tiles * tile, D), f32),
        scratch_shapes=[pltpu.VMEM((tile * chunks, LANES), f32), pltpu.SemaphoreType.DMA],
        compiler_params=pltpu.CompilerParams(
            dimension_semantics=("arbitrary",), vmem_limit_bytes=VMEM_LIMIT),
    )(pos3, ys)


def _experts_kernel(ea_ref, eb_ref, nv_ref, xs_ref, g2_ref, gf_ref, wr_ref,
                    wgu_a_ref, wd_a_ref, wgu_b_ref, wd_b_ref, ys_ref, *, final_norm):
    D = g2_ref.shape[1]
    chunks = D // LANES
    n = xs_ref.shape[0] // chunks
    d_ff = wd_a_ref.shape[0]
    i = pl.program_id(0)
    active = nv_ref[i] > 0

    @pl.when(active)
    def _():
        x = jnp.concatenate([xs_ref[pl.ds(j, n, stride=chunks), :] for j in range(chunks)], axis=1)
        ms = jnp.mean(x * x, axis=-1, keepdims=True)
        h = (x * lax.rsqrt(ms + EPS) * g2_ref[...]).astype(bf16)

        logits = jnp.dot(h, wr_ref[...], preferred_element_type=f32)
        lane = lax.broadcasted_iota(jnp.int32, logits.shape, 1)
        pick = lambda idx: jnp.sum(jnp.where(lane == idx, logits, 0.0), axis=1, keepdims=True)
        ea, eb = ea_ref[i], eb_ref[i]
        l_group = pick(ea // EXPERTS_PER_GROUP)
        p_group = 1.0 / jnp.sum(jnp.where(lane < N_GROUPS, jnp.exp(logits - l_group), 0.0),
                                axis=1, keepdims=True)
        l_a, l_b = pick(N_GROUPS + ea), pick(N_GROUPS + eb)
        gates = (p_group / (1.0 + jnp.exp(l_b - l_a)), p_group / (1.0 + jnp.exp(l_a - l_b)))

        y = x
        for gate, wgu_ref, wd_ref in ((gates[0], wgu_a_ref, wd_a_ref), (gates[1], wgu_b_ref, wd_b_ref)):
            gu = jnp.dot(h, wgu_ref[...], preferred_element_type=f32)
            a = gu[:, :d_ff]
            act = (a * jax.nn.sigmoid(a) * gu[:, d_ff:]).astype(bf16)
            y = y + gate * jnp.dot(act, wd_ref[...], preferred_element_type=f32)
        if final_norm:
            ms = jnp.mean(y * y, axis=-1, keepdims=True)
            y = y * lax.rsqrt(ms + EPS) * gf_ref[...]
        for j in range(chunks):
            ys_ref[pl.ds(j, n, stride=chunks), :] = y[:, j * LANES:(j + 1) * LANES]

    @pl.when(jnp.logical_not(active))
    def _():
        ys_ref[...] = jnp.zeros_like(ys_ref)


def _experts(xs, blk_ea, blk_eb, blk_valid, g2, gf, wr, wgu, wd, final_norm):
    D = g2.shape[1]
    chunks = D // LANES
    d_ff = wd.shape[1]
    blk = MOE_BLOCK * chunks
    idx = lambda f: (lambda i, ea, eb, nv: f(i, ea, eb))
    const = lambda a: pl.BlockSpec(a.shape, idx(lambda i, ea, eb: (0, 0)))
    return pl.pallas_call(
        functools.partial(_experts_kernel, final_norm=final_norm),
        grid_spec=pltpu.PrefetchScalarGridSpec(
            num_scalar_prefetch=3,
            grid=(xs.shape[0] // blk,),
            in_specs=[
                pl.BlockSpec((blk, LANES), idx(lambda i, ea, eb: (i, 0))),
                const(g2), const(gf), const(wr),
                pl.BlockSpec((None, D, 2 * d_ff), idx(lambda i, ea, eb: (ea[i], 0, 0))),
                pl.BlockSpec((None, d_ff, D), idx(lambda i, ea, eb: (ea[i], 0, 0))),
                pl.BlockSpec((None, D, 2 * d_ff), idx(lambda i, ea, eb: (eb[i], 0, 0))),
                pl.BlockSpec((None, d_ff, D), idx(lambda i, ea, eb: (eb[i], 0, 0))),
            ],
            out_specs=pl.BlockSpec((blk, LANES), idx(lambda i, ea, eb: (i, 0))),
        ),
        out_shape=jax.ShapeDtypeStruct(xs.shape, f32),
        compiler_params=pltpu.CompilerParams(
            dimension_semantics=("arbitrary",), vmem_limit_bytes=VMEM_LIMIT),
    )(blk_ea, blk_eb, blk_valid, xs, g2, gf, wr, wgu, wd, wgu, wd)


def _moe(x1, cls_t, counts, g2, gf, wr, wgu, wd, final_norm):
    T, D = x1.shape
    n_rows = T + N_CLASSES * MOE_BLOCK
    pad_starts, blk_ea, blk_eb, blk_valid = _block_plan(counts[0, :N_CLASSES], n_rows // MOE_BLOCK)
    pos3 = _positions(cls_t, pad_starts).reshape(T // TOKEN_TILE, 1, TOKEN_TILE)
    xs = _dispatch(pos3, x1, n_rows)
    ys = _experts(xs, blk_ea, blk_eb, blk_valid, g2, gf, wr, wgu, wd, final_norm)
    return _combine(pos3, ys, D)


def kernel(x_prompt, x_sample, norm1_g, w_in, lambda_q1, lambda_k1, lambda_q2, lambda_k2, subln_g,
           w_out, norm2_g, w_router_group, w_router_expert, w_gate, w_up, w_down, final_norm_g):
    n_prompt = x_prompt.shape[0]
    x = jnp.concatenate([x_prompt, x_sample], axis=0)
    nb, seq, D = x.shape
    depth = w_in.shape[0]
    x2d = x.reshape(nb * seq, D)
    cos_np, sin_np = _rope_tables(seq)
    cos, sin = jnp.asarray(cos_np), jnp.asarray(sin_np)
    row = lambda v: v.reshape(1, -1).astype(f32)

    for l in range(depth):
        qa, ka, va, qb, kb, vb = _inproj(x2d, row(norm1_g[l]), w_in[l].astype(bf16), cos, sin, seq)
        to_seq = lambda t: t.reshape(nb, seq, t.shape[-1])
        oa = _mixer_a(to_seq(qa), to_seq(ka), to_seq(va))
        lam_init = 0.8 - 0.6 * math.exp(-0.3 * l)
        lam = (jnp.exp(jnp.sum(lambda_q1[l].astype(f32) * lambda_k1[l].astype(f32)))
               - jnp.exp(jnp.sum(lambda_q2[l].astype(f32) * lambda_k2[l].astype(f32)))
               + lam_init).reshape(1)
        ob = _mixer_b(lam, to_seq(qb), to_seq(kb), to_seq(vb), row(subln_g[l]), 1.0 - lam_init)

        w_router = jnp.concatenate([w_router_group[l], w_router_expert[l]], axis=1).astype(f32)
        w_router = jnp.pad(w_router, ((0, 0), (0, LANES - w_router.shape[1])))
        wr_hi = w_router.astype(bf16)
        wr_lo = (w_router - wr_hi.astype(f32)).astype(bf16)
        wo = w_out[l].astype(bf16)
        x1, cls_t, counts = _outproj(oa.reshape(nb * seq, WIDTH_A), ob.reshape(nb * seq, WIDTH_B),
                                     x2d, wo[:WIDTH_A], wo[WIDTH_A:], row(norm2_g[l]), wr_hi, wr_lo)

        wgu = jnp.concatenate([w_gate[l], w_up[l]], axis=-1).astype(bf16)
        x2d = _moe(x1, cls_t, counts, row(norm2_g[l]), row(final_norm_g), wr_hi, wgu,
                   w_down[l].astype(bf16), final_norm=(l == depth - 1))

    y = x2d.reshape(nb, seq, D)
    return y[:n_prompt], y[n_prompt:]
```

```python
import functools
import math

import numpy as np
import jax
import jax.numpy as jnp
from jax import lax
from jax.experimental import pallas as pl
from jax.experimental.pallas import tpu as pltpu

HEAD_DIM = 64
N_HEADS_A = 8
WIDTH_A = N_HEADS_A * HEAD_DIM
DIL_PATTERNS = ((128, 1), (512, 4), (2048, 16))
N_HEADS_B = 4
QK_WIDTH_B = N_HEADS_B * 2 * HEAD_DIM
V_DIM_B = 2 * HEAD_DIM
WIDTH_B = N_HEADS_B * V_DIM_B
IN_COLS = 3 * WIDTH_A + 2 * QK_WIDTH_B + WIDTH_B
ROPE_THETA = 10000.0
EPS = 1e-6
N_GROUPS = 4
EXPERTS_PER_GROUP = 8
N_EXPERTS = N_GROUPS * EXPERTS_PER_GROUP
PAIRS_PER_GROUP = EXPERTS_PER_GROUP * (EXPERTS_PER_GROUP - 1) // 2
N_CLASSES = N_GROUPS * PAIRS_PER_GROUP
NEG = -1e30

LANES = 128
SUBLANES = 8
TOKEN_TILE = 512
Q_TILE_B = 256
KEY_CHUNK_B = 1024
BAND_BLOCK = 128
BAND_RADIUS = 64
BAND_WINDOW = BAND_BLOCK + 2 * BAND_RADIUS
BAND_UNROLL = 8
MOE_BLOCK = 256
DMA_UNROLL = 8
VMEM_LIMIT = 56 * 1024 * 1024

QK_SCALE = HEAD_DIM ** -0.5 * math.log2(math.e)

f32 = jnp.float32
bf16 = jnp.bfloat16


def _rope_tables(seq):
    half = HEAD_DIM // 2
    inv = ROPE_THETA ** (-np.arange(half, dtype=np.float64) / half)
    lane = np.arange(LANES)
    ang = np.arange(seq, dtype=np.float64)[:, None] * inv[lane % half][None, :]
    sign = np.where((lane % HEAD_DIM) < half, -1.0, 1.0)[None, :]
    return np.cos(ang).astype(np.float32), (np.sin(ang) * sign).astype(np.float32)


def _inproj_kernel(x_ref, g_ref, w_ref, cos_ref, sin_ref,
                   qa_ref, ka_ref, va_ref, qb_ref, kb_ref, vb_ref):
    x = x_ref[...]
    ms = jnp.mean(x * x, axis=-1, keepdims=True)
    h = (x * lax.rsqrt(ms + EPS) * g_ref[...]).astype(bf16)
    p = jnp.dot(h, w_ref[...], preferred_element_type=f32)
    cos = cos_ref[...]
    sin = sin_ref[...]
    lane = lax.broadcasted_iota(jnp.int32, cos.shape, 1)
    first_half = (lane & (HEAD_DIM - 1)) < (HEAD_DIM // 2)

    def rope(t):
        rot = jnp.where(first_half, pltpu.roll(t, LANES - HEAD_DIM // 2, 1),
                        pltpu.roll(t, HEAD_DIM // 2, 1))
        return t * cos + rot * sin

    plan = ((qa_ref, 0, True, QK_SCALE), (ka_ref, WIDTH_A, True, None),
            (va_ref, 2 * WIDTH_A, False, None),
            (qb_ref, 3 * WIDTH_A, True, QK_SCALE), (kb_ref, 3 * WIDTH_A + QK_WIDTH_B, True, None),
            (vb_ref, 3 * WIDTH_A + 2 * QK_WIDTH_B, False, None))
    for ref, off, rotary, scale in plan:
        for c in range(ref.shape[1] // LANES):
            t = p[:, off + c * LANES: off + (c + 1) * LANES]
            if rotary:
                t = rope(t)
            if scale is not None:
                t = t * scale
            ref[:, c * LANES:(c + 1) * LANES] = t.astype(bf16)


def _inproj(x2d, g, w, cos, sin, seq):
    T, D = x2d.shape
    tiles_per_seq = seq // TOKEN_TILE
    out = jax.ShapeDtypeStruct((T, WIDTH_A), bf16)
    row_spec = lambda width: pl.BlockSpec((TOKEN_TILE, width), lambda i: (i, 0))
    return pl.pallas_call(
        _inproj_kernel,
        grid=(T // TOKEN_TILE,),
        in_specs=[
            row_spec(D),
            pl.BlockSpec((1, D), lambda i: (0, 0)),
            pl.BlockSpec((D, IN_COLS), lambda i: (0, 0)),
            pl.BlockSpec((TOKEN_TILE, LANES), lambda i: (i % tiles_per_seq, 0)),
            pl.BlockSpec((TOKEN_TILE, LANES), lambda i: (i % tiles_per_seq, 0)),
        ],
        out_specs=[row_spec(WIDTH_A)] * 6,
        out_shape=[out] * 6,
        compiler_params=pltpu.CompilerParams(
            dimension_semantics=("arbitrary",), vmem_limit_bytes=VMEM_LIMIT),
    )(x2d, g, w, cos, sin)


def _band_block(q, k, v, bias):
    nq = q.shape[0]
    lane = lax.broadcasted_iota(jnp.int32, (nq, LANES), 1)
    head0 = lane < HEAD_DIM
    parts = []
    for sel in (head0, jnp.logical_not(head0)):
        qm = jnp.where(sel, q, jnp.zeros_like(q))
        s = lax.dot_general(qm, k, (((1,), (1,)), ((), ())), preferred_element_type=f32) + bias
        m = jnp.max(s, axis=1, keepdims=True)
        e = jnp.exp2(s - m)
        l = jnp.sum(e, axis=1, keepdims=True)
        o = jnp.dot(e.astype(bf16), v, preferred_element_type=f32)
        parts.append((o, m, l))
    (o0, m0, l0), (o1, m1, l1) = parts
    return (jnp.where(head0, o0, o1),
            jnp.where(head0, jnp.broadcast_to(m0, (nq, LANES)), jnp.broadcast_to(m1, (nq, LANES))),
            jnp.where(head0, jnp.broadcast_to(l0, (nq, LANES)), jnp.broadcast_to(l1, (nq, LANES))))


def _mixer_a_kernel(q_ref, k_ref, v_ref, o_ref, qf, kf, vf, q2, k2, v2, qd, kd, vd, oc, mc, lc,
                    acc_s, m_s, l_s, bias_s):
    seq = q_ref.shape[0]
    n_blocks = seq // BAND_BLOCK

    rel = (lax.broadcasted_iota(jnp.int32, (BAND_BLOCK, BAND_WINDOW), 0)
           - lax.broadcasted_iota(jnp.int32, (BAND_BLOCK, BAND_WINDOW), 1))
    for j in range(3):
        bias_s[j] = jnp.where(jnp.abs(rel + j * BAND_RADIUS) <= BAND_RADIUS, 0.0, NEG)

    dils = [d for _, d in DIL_PATTERNS]

    def refine(i):
        ratio = dils[i] // dils[i - 1]
        len_prev, len_cur = seq // dils[i - 1], seq // dils[i]
        return [(pl.ds(s * len_prev + r, len_cur, stride=ratio),
                 pl.ds((s * ratio + r) * len_cur, len_cur))
                for s in range(dils[i - 1]) for r in range(ratio)]

    f32_bufs = ((qf, kf, vf), (q2, k2, v2))
    state = (acc_s, m_s, l_s)
    for i, dil in enumerate(dils):
        cls_len = seq // dil
        blocks_per_cls = cls_len // BAND_BLOCK
        if i == 0:
            src, dst = (q_ref, k_ref, v_ref), state
        else:
            prev_f32, cur_f32 = f32_bufs[(i - 1) % 2], f32_bufs[i % 2]
            if i == 1:
                for buf, ref in zip(prev_f32, (q_ref, k_ref, v_ref)):
                    buf[...] = ref[...].astype(f32)
            for strided, rows in refine(i):
                for major, cur, prev in zip((qd, kd, vd), cur_f32, prev_f32):
                    t = prev[strided, :]
                    major[rows, :] = t.astype(bf16)
                    if i + 1 < len(dils):
                        cur[rows, :] = t
            src, dst = (qd, kd, vd), (oc, mc, lc)

        def body(it, carry, src=src, dst=dst, cls_len=cls_len, blocks_per_cls=blocks_per_cls):
            for u in range(BAND_UNROLL):
                b = it * BAND_UNROLL + u
                blk = b % blocks_per_cls
                base = (b // blocks_per_cls) * cls_len
                q0 = blk * BAND_BLOCK
                ws = jnp.clip(q0 - BAND_RADIUS, 0, cls_len - BAND_WINDOW)
                q_rows = pl.ds(pl.multiple_of(base + q0, BAND_BLOCK), BAND_BLOCK)
                k_rows = pl.ds(pl.multiple_of(base + ws, BAND_RADIUS), BAND_WINDOW)
                o, m, l = _band_block(src[0][q_rows, :], src[1][k_rows, :], src[2][k_rows, :],
                                      bias_s[(q0 - ws) // BAND_RADIUS])
                dst[0][q_rows, :] = o
                dst[1][q_rows, :] = m
                dst[2][q_rows, :] = l
            return carry

        lax.fori_loop(0, n_blocks // BAND_UNROLL, body, 0)

        if i > 0:
            new_state = f32_bufs[(i - 1) % 2]
            for strided, rows in refine(i):
                m_old = state[1][strided, :]
                m_pat = mc[rows, :]
                m_new = jnp.maximum(m_old, m_pat)
                a_old = jnp.exp2(m_old - m_new)
                a_pat = jnp.exp2(m_pat - m_new)
                new_state[0][rows, :] = state[0][strided, :] * a_old + oc[rows, :] * a_pat
                new_state[2][rows, :] = state[2][strided, :] * a_old + lc[rows, :] * a_pat
                new_state[1][rows, :] = m_new
            state = new_state

    hops = (acc_s, m_s, l_s)
    assert len(dils) == len(hops) and state[0] is not acc_s
    hops[0][...] = state[0][...] / state[2][...]
    for hop, i in enumerate(reversed(range(1, len(dils)))):
        for strided, rows in refine(i):
            hops[hop + 1][strided, :] = hops[hop][rows, :]
    o_ref[...] = hops[len(dils) - 1][...].astype(o_ref.dtype)


def _mixer_a(q, k, v):
    nb, seq, width = q.shape
    spec = pl.BlockSpec((None, seq, LANES), lambda b, p: (b, 0, p))
    tile_f32 = pltpu.VMEM((seq, LANES), f32)
    tile_bf16 = pltpu.VMEM((seq, LANES), bf16)
    return pl.pallas_call(
        _mixer_a_kernel,
        grid=(nb, width // LANES),
        in_specs=[spec, spec, spec],
        out_specs=spec,
        out_shape=jax.ShapeDtypeStruct((nb, seq, width), bf16),
        scratch_shapes=[tile_f32] * 6 + [tile_bf16] * 3 + [tile_f32] * 6
        + [pltpu.VMEM((3, BAND_BLOCK, BAND_WINDOW), f32)],
        compiler_params=pltpu.CompilerParams(
            dimension_semantics=("arbitrary", "arbitrary"), vmem_limit_bytes=VMEM_LIMIT),
    )(q, k, v)


def _mixer_b_kernel(lam_ref, q_ref, k_ref, v_ref, g_ref, o_ref,
                    s_even, s_odd, e_even, e_odd, m_fin, coef, *, out_scale):
    step = pl.program_id(0)
    n_q, seq = s_even.shape[1:]
    kc = KEY_CHUNK_B

    @pl.when(step == 0)
    def _():
        for ref in (s_even, s_odd, e_even, e_odd, m_fin, coef):
            ref[...] = jnp.zeros_like(ref)

    def stages(s_w, s_r, e_w, e_r):
        q = q_ref[...]
        lane = lax.broadcasted_iota(jnp.int32, q.shape, 1)
        first = lane < HEAD_DIM
        zero = jnp.zeros_like(q)
        q_sub = (jnp.where(first, q, zero), jnp.where(first, zero, q))
        nt = (((1,), (1,)), ((), ()))
        m_prev = (m_fin[0], m_fin[1])
        c_prev = (coef[0], coef[1])
        row_max = [None, None]
        row_sum = [None, None]
        acc = None
        for c in range(seq // kc):
            keys = slice(c * kc, (c + 1) * kc)
            k_c = k_ref[keys, :]
            for sub in range(2):
                s = lax.dot_general(q_sub[sub], k_c, nt, preferred_element_type=f32)
                s_w[sub, :, keys] = s
                part = functools.reduce(
                    jnp.maximum, [s[:, j * LANES:(j + 1) * LANES] for j in range(kc // LANES)])
                row_max[sub] = part if c == 0 else jnp.maximum(row_max[sub], part)
            lane_groups = [slice(c * kc + j * LANES, c * kc + (j + 1) * LANES)
                           for j in range(kc // LANES)]
            for sub in range(2):
                chunk_sum = None
                for cols in lane_groups:
                    e = jnp.exp2(s_r[sub, :, cols] - m_prev[sub])
                    chunk_sum = e if chunk_sum is None else chunk_sum + e
                    e_w[sub, :, cols] = e.astype(bf16)
                row_sum[sub] = chunk_sum if c == 0 else row_sum[sub] + chunk_sum
            a = jnp.concatenate([e_r[0, :, cols] * c_prev[0] - e_r[1, :, cols] * c_prev[1]
                                 for cols in lane_groups], axis=1)
            pv = jnp.dot(a, v_ref[keys, :], preferred_element_type=f32)
            acc = pv if c == 0 else acc + pv

        for sub in range(2):
            m_fin[sub] = jnp.broadcast_to(jnp.max(row_max[sub], axis=1, keepdims=True), (n_q, LANES))
        l0 = jnp.sum(row_sum[0], axis=1, keepdims=True)
        l1 = jnp.sum(row_sum[1], axis=1, keepdims=True)
        coef[0] = jnp.broadcast_to(1.0 / l0, (n_q, LANES)).astype(bf16)
        coef[1] = jnp.broadcast_to(lam_ref[0] / l1, (n_q, LANES)).astype(bf16)
        ms = jnp.mean(acc * acc, axis=-1, keepdims=True)
        o_ref[...] = (acc * lax.rsqrt(ms + EPS) * g_ref[...] * out_scale).astype(o_ref.dtype)

    @pl.when(step % 2 == 0)
    def _():
        stages(s_even, s_odd, e_even, e_odd)

    @pl.when(step % 2 == 1)
    def _():
        stages(s_odd, s_even, e_odd, e_even)


def _mixer_b(lam, q, k, v, g, out_scale):
    nb, seq, width = q.shape
    heads = width // LANES
    q_tiles = seq // Q_TILE_B
    n_tiles = nb * heads * q_tiles

    def tile(t):
        return t // (heads * q_tiles), (t // q_tiles) % heads, t % q_tiles

    def score_tile(step):
        return tile(jnp.minimum(step, n_tiles - 1))

    def value_tile(step):
        return tile(jnp.clip(step - 2, 0, n_tiles - 1))

    def q_map(step, lam):
        b, h, i = score_tile(step)
        return b, i, h

    def k_map(step, lam):
        b, h, _ = score_tile(step)
        return b, 0, h

    def v_map(step, lam):
        b, h, _ = value_tile(step)
        return b, 0, h

    def o_map(step, lam):
        b, h, i = value_tile(step)
        return b, i, h

    return pl.pallas_call(
        functools.partial(_mixer_b_kernel, out_scale=out_scale),
        grid_spec=pltpu.PrefetchScalarGridSpec(
            num_scalar_prefetch=1,
            grid=(n_tiles + 2,),
            in_specs=[pl.BlockSpec((None, Q_TILE_B, LANES), q_map),
                      pl.BlockSpec((None, seq, LANES), k_map),
                      pl.BlockSpec((None, seq, LANES), v_map),
                      pl.BlockSpec((1, LANES), lambda step, lam: (0, 0))],
            out_specs=pl.BlockSpec((None, Q_TILE_B, LANES), o_map),
            scratch_shapes=[pltpu.VMEM((2, Q_TILE_B, seq), f32)] * 2
            + [pltpu.VMEM((2, Q_TILE_B, seq), bf16)] * 2
            + [pltpu.VMEM((2, Q_TILE_B, LANES), f32), pltpu.VMEM((2, Q_TILE_B, LANES), bf16)],
        ),
        out_shape=jax.ShapeDtypeStruct((nb, seq, width), bf16),
        compiler_params=pltpu.CompilerParams(
            dimension_semantics=("arbitrary",), vmem_limit_bytes=VMEM_LIMIT),
    )(lam, q, k, v, g)


def _outproj_kernel(oa_ref, ob_ref, x_ref, wa_ref, wb_ref, g_ref, wr_hi_ref, wr_lo_ref,
                    x1_ref, cls_t_ref, counts_ref):
    x1 = (x_ref[...]
          + jnp.dot(oa_ref[...], wa_ref[...], preferred_element_type=f32)
          + jnp.dot(ob_ref[...], wb_ref[...], preferred_element_type=f32))
    x1_ref[...] = x1
    ms = jnp.mean(x1 * x1, axis=-1, keepdims=True)
    h = x1 * lax.rsqrt(ms + EPS) * g_ref[...]
    h_hi = h.astype(bf16)
    h_lo = (h - h_hi.astype(f32)).astype(bf16)
    logits = (jnp.dot(h_hi, wr_hi_ref[...], preferred_element_type=f32)
              + jnp.dot(h_hi, wr_lo_ref[...], preferred_element_type=f32)
              + jnp.dot(h_lo, wr_hi_ref[...], preferred_element_type=f32))

    lane = lax.broadcasted_iota(jnp.int32, logits.shape, 1)
    lane_f = lane.astype(f32)

    def first_argmax(vals):
        top = jnp.max(vals, axis=1, keepdims=True)
        first = jnp.min(jnp.where(vals == top, lane_f, float(LANES)), axis=1, keepdims=True)
        return first.astype(jnp.int32)

    g_idx = first_argmax(jnp.where(lane < N_GROUPS, logits, NEG))
    group_first = N_GROUPS + g_idx * EXPERTS_PER_GROUP
    in_group = (lane >= group_first) & (lane < group_first + EXPERTS_PER_GROUP)
    e_logits = jnp.where(in_group, logits, NEG)
    i1 = first_argmax(e_logits)
    i2 = first_argmax(jnp.where(lane == i1, NEG, e_logits))
    lo = jnp.minimum(i1, i2) - group_first
    hi = jnp.maximum(i1, i2) - group_first
    pair = jnp.right_shift(lo * (2 * EXPERTS_PER_GROUP - 1 - lo), 1) + (hi - lo - 1)
    cls = g_idx * PAIRS_PER_GROUP + pair
    cls_lanes = jnp.broadcast_to(cls.astype(f32), logits.shape)
    cls_t_ref[...] = jnp.transpose(cls_lanes)[:SUBLANES, :]

    @pl.when(pl.program_id(0) == 0)
    def _():
        counts_ref[...] = jnp.zeros_like(counts_ref)

    counts_ref[...] += jnp.sum(jnp.where(lane == cls, 1.0, 0.0), axis=0, keepdims=True)


def _outproj(oa, ob, x2d, wa, wb, g, wr_hi, wr_lo):
    T, D = x2d.shape
    row_spec = lambda width: pl.BlockSpec((TOKEN_TILE, width), lambda i: (i, 0))
    full = lambda a: pl.BlockSpec(a.shape, lambda i: (0, 0))
    return pl.pallas_call(
        _outproj_kernel,
        grid=(T // TOKEN_TILE,),
        in_specs=[row_spec(WIDTH_A), row_spec(WIDTH_B), row_spec(D), full(wa), full(wb), full(g),
                  full(wr_hi), full(wr_lo)],
        out_specs=[row_spec(D),
                   pl.BlockSpec((SUBLANES, TOKEN_TILE), lambda i: (0, i)),
                   pl.BlockSpec((1, LANES), lambda i: (0, 0))],
        out_shape=[jax.ShapeDtypeStruct((T, D), f32),
                   jax.ShapeDtypeStruct((SUBLANES, T), f32),
                   jax.ShapeDtypeStruct((1, LANES), f32)],
        compiler_params=pltpu.CompilerParams(
            dimension_semantics=("arbitrary",), vmem_limit_bytes=VMEM_LIMIT),
    )(oa, ob, x2d, wa, wb, g, wr_hi, wr_lo)


def _pair_tables():
    lo, hi = [], []
    for g in range(N_GROUPS):
        for a in range(EXPERTS_PER_GROUP):
            for b in range(a + 1, EXPERTS_PER_GROUP):
                lo.append(g * EXPERTS_PER_GROUP + a)
                hi.append(g * EXPERTS_PER_GROUP + b)
    return np.asarray(lo, np.int32), np.asarray(hi, np.int32)


def _block_plan(counts, n_blocks):
    counts = counts.astype(jnp.int32)
    padded = (counts + MOE_BLOCK - 1) // MOE_BLOCK * MOE_BLOCK
    pad_ends = jnp.cumsum(padded)
    pad_starts = pad_ends - padded
    blk_start = jnp.arange(n_blocks, dtype=jnp.int32) * MOE_BLOCK
    blk_cls = jnp.minimum(jnp.sum(pad_ends[None, :] <= blk_start[:, None], axis=1),
                          N_CLASSES - 1).astype(jnp.int32)
    onehot = (blk_cls[:, None] == jnp.arange(N_CLASSES, dtype=jnp.int32)[None, :]).astype(jnp.int32)
    used = jnp.sum(onehot * (counts - (blk_start[:, None] - pad_starts[None, :])), axis=1)
    blk_valid = jnp.where(blk_start < pad_ends[-1], jnp.clip(used, 0, MOE_BLOCK), 0).astype(jnp.int32)
    lo_tab, hi_tab = _pair_tables()
    blk_ea = jnp.sum(onehot * jnp.asarray(lo_tab)[None, :], axis=1).astype(jnp.int32)
    blk_eb = jnp.sum(onehot * jnp.asarray(hi_tab)[None, :], axis=1).astype(jnp.int32)
    return pad_starts, blk_ea, blk_eb, blk_valid


def _positions_kernel(cls_t_ref, start_ref, pos_ref, base_s):
    n = cls_t_ref.shape[1]

    @pl.when(pl.program_id(0) == 0)
    def _():
        base_s[...] = start_ref[...]

    cls = cls_t_ref[0:1, :].astype(jnp.int32)
    onehot = lax.broadcasted_iota(jnp.int32, (LANES, n), 0) == cls
    earlier = (lax.broadcasted_iota(jnp.int32, (n, n), 0)
               < lax.broadcasted_iota(jnp.int32, (n, n), 1))
    onehot_f = jnp.where(onehot, 1.0, 0.0)
    rank = jnp.dot(onehot_f.astype(bf16), jnp.where(earlier, 1.0, 0.0).astype(bf16),
                   preferred_element_type=f32)
    base = base_s[...]
    pos = jnp.sum(jnp.where(onehot, rank + base, 0.0), axis=0, keepdims=True)
    pos_ref[...] = pos.astype(jnp.int32)
    base_s[...] = base + jnp.sum(onehot_f, axis=1, keepdims=True)


def _positions(cls_t, pad_starts):
    T = cls_t.shape[1]
    start_col = jnp.zeros((LANES, 1), f32).at[:N_CLASSES, 0].set(pad_starts.astype(f32))
    return pl.pallas_call(
        _positions_kernel,
        grid=(T // TOKEN_TILE,),
        in_specs=[pl.BlockSpec((SUBLANES, TOKEN_TILE), lambda i: (0, i)),
                  pl.BlockSpec((LANES, 1), lambda i: (0, 0))],
        out_specs=pl.BlockSpec((1, TOKEN_TILE), lambda i: (0, i)),
        out_shape=jax.ShapeDtypeStruct((1, T), jnp.int32),
        scratch_shapes=[pltpu.VMEM((LANES, 1), f32)],
        compiler_params=pltpu.CompilerParams(
            dimension_semantics=("arbitrary",), vmem_limit_bytes=VMEM_LIMIT),
    )(cls_t, start_col)


def _token_copy_loops(make_copy, n):
    def start(r, c):
        make_copy(r).start()
        return c
    lax.fori_loop(0, n, start, 0, unroll=DMA_UNROLL)

    def wait(r, c):
        make_copy(r).wait()
        return c
    lax.fori_loop(0, n, wait, 0, unroll=DMA_UNROLL)


def _dispatch_kernel(pos_ref, x_ref, init_hbm, xs_hbm, stage, sem):
    del init_hbm
    n, D = x_ref.shape
    chunks = D // LANES
    for j in range(chunks):
        stage[pl.ds(j, n, stride=chunks), :] = x_ref[:, j * LANES:(j + 1) * LANES]

    def copy(r):
        src = stage.at[pl.ds(pl.multiple_of(r * chunks, chunks), chunks)]
        dst = xs_hbm.at[pl.ds(pl.multiple_of(pos_ref[0, r] * chunks, chunks), chunks)]
        return pltpu.make_async_copy(src, dst, sem)

    _token_copy_loops(copy, n)


def _dispatch(pos3, x1, n_rows):
    n_tiles, _, tile = pos3.shape
    D = x1.shape[1]
    chunks = D // LANES
    shape = jax.ShapeDtypeStruct((n_rows * chunks, LANES), f32)
    return pl.pallas_call(
        _dispatch_kernel,
        grid=(n_tiles,),
        in_specs=[pl.BlockSpec((None, 1, tile), lambda i: (i, 0, 0), memory_space=pltpu.SMEM),
                  pl.BlockSpec((tile, D), lambda i: (i, 0)),
                  pl.BlockSpec(memory_space=pl.ANY)],
        out_specs=pl.BlockSpec(memory_space=pl.ANY),
        out_shape=shape,
        scratch_shapes=[pltpu.VMEM((tile * chunks, LANES), f32), pltpu.SemaphoreType.DMA],
        input_output_aliases={2: 0},
        compiler_params=pltpu.CompilerParams(
            dimension_semantics=("arbitrary",), vmem_limit_bytes=VMEM_LIMIT),
    )(pos3, x1, jnp.zeros(shape.shape, f32))


def _combine_kernel(pos_ref, ys_hbm, *rest, split):
    outs, (stage, sem) = rest[:-2], rest[-2:]
    n, D = outs[0].shape
    chunks = D // LANES

    def copy(r):
        src = ys_hbm.at[pl.ds(pl.multiple_of(pos_ref[0, r] * chunks, chunks), chunks)]
        dst = stage.at[pl.ds(pl.multiple_of(r * chunks, chunks), chunks)]
        return pltpu.make_async_copy(src, dst, sem)

    _token_copy_loops(copy, n)

    def write(o_ref):
        for j in range(chunks):
            o_ref[:, j * LANES:(j + 1) * LANES] = stage[pl.ds(j, n, stride=chunks), :]

    if split is None:
        write(outs[0])
    else:
        pl.when(pl.program_id(0) < split)(lambda: write(outs[0]))
        pl.when(pl.program_id(0) >= split)(lambda: write(outs[1]))


def _combine(pos3, ys, D, split=None):
    n_tiles, _, tile = pos3.shape
    chunks = D // LANES
    if split is None:
        out_specs = pl.BlockSpec((tile, D), lambda i: (i, 0))
        out_shape = jax.ShapeDtypeStruct((n_tiles * tile, D), f32)
    else:
        out_specs = [pl.BlockSpec((tile, D), lambda i: (jnp.minimum(i, split - 1), 0)),
                     pl.BlockSpec((tile, D), lambda i: (jnp.maximum(i - split, 0), 0))]
        out_shape = [jax.ShapeDtypeStruct((split * tile, D), f32),
                     jax.ShapeDtypeStruct(((n_tiles - split) * tile, D), f32)]
    return pl.pallas_call(
        functools.partial(_combine_kernel, split=split),
        grid=(n_tiles,),
        in_specs=[pl.BlockSpec((None, 1, tile), lambda i: (i, 0, 0), memory_space=pltpu.SMEM),
                  pl.BlockSpec(memory_space=pl.ANY)],
        out_specs=out_specs,
        out_shape=out_shape,
        scratch_shapes=[pltpu.VMEM((tile * chunks, LANES), f32), pltpu.SemaphoreType.DMA],
        compiler_params=pltpu.CompilerParams(
            dimension_semantics=("arbitrary",), vmem_limit_bytes=VMEM_LIMIT),
    )(pos3, ys)


def _experts_kernel(ea_ref, eb_ref, nv_ref, xs_ref, g2_ref, gf_ref, wr_ref,
                    wgu_a_ref, wd_a_ref, wgu_b_ref, wd_b_ref, ys_ref, *, final_norm):
    D = g2_ref.shape[1]
    chunks = D // LANES
    n = xs_ref.shape[0] // chunks
    d_ff = wd_a_ref.shape[0]
    i = pl.program_id(0)
    active = nv_ref[i] > 0

    @pl.when(active)
    def _():
        x = jnp.concatenate([xs_ref[pl.ds(j, n, stride=chunks), :] for j in range(chunks)], axis=1)
        ms = jnp.mean(x * x, axis=-1, keepdims=True)
        h = (x * lax.rsqrt(ms + EPS) * g2_ref[...]).astype(bf16)

        logits = jnp.dot(h, wr_ref[...], preferred_element_type=f32)
        lane = lax.broadcasted_iota(jnp.int32, logits.shape, 1)
        pick = lambda idx: jnp.sum(jnp.where(lane == idx, logits, 0.0), axis=1, keepdims=True)
        ea, eb = ea_ref[i], eb_ref[i]
        l_group = pick(ea // EXPERTS_PER_GROUP)
        p_group = 1.0 / jnp.sum(jnp.where(lane < N_GROUPS, jnp.exp(logits - l_group), 0.0),
                                axis=1, keepdims=True)
        l_a, l_b = pick(N_GROUPS + ea), pick(N_GROUPS + eb)
        gates = (p_group / (1.0 + jnp.exp(l_b - l_a)), p_group / (1.0 + jnp.exp(l_a - l_b)))

        y = x
        for gate, wgu_ref, wd_ref in ((gates[0], wgu_a_ref, wd_a_ref), (gates[1], wgu_b_ref, wd_b_ref)):
            gu = jnp.dot(h, wgu_ref[...], preferred_element_type=f32)
            a = gu[:, :d_ff]
            act = (a * jax.nn.sigmoid(a) * gu[:, d_ff:]).astype(bf16)
            y = y + gate * jnp.dot(act, wd_ref[...], preferred_element_type=f32)
        if final_norm:
            ms = jnp.mean(y * y, axis=-1, keepdims=True)
            y = y * lax.rsqrt(ms + EPS) * gf_ref[...]
        for j in range(chunks):
            ys_ref[pl.ds(j, n, stride=chunks), :] = y[:, j * LANES:(j + 1) * LANES]

    @pl.when(jnp.logical_not(active))
    def _():
        ys_ref[...] = jnp.zeros_like(ys_ref)


def _experts(xs, blk_ea, blk_eb, blk_valid, g2, gf, wr, wgu, wd, final_norm):
    D = g2.shape[1]
    chunks = D // LANES
    d_ff = wd.shape[1]
    blk = MOE_BLOCK * chunks
    idx = lambda f: (lambda i, ea, eb, nv: f(i, ea, eb))
    const = lambda a: pl.BlockSpec(a.shape, idx(lambda i, ea, eb: (0, 0)))
    return pl.pallas_call(
        functools.partial(_experts_kernel, final_norm=final_norm),
        grid_spec=pltpu.PrefetchScalarGridSpec(
            num_scalar_prefetch=3,
            grid=(xs.shape[0] // blk,),
            in_specs=[
                pl.BlockSpec((blk, LANES), idx(lambda i, ea, eb: (i, 0))),
                const(g2), const(gf), const(wr),
                pl.BlockSpec((None, D, 2 * d_ff), idx(lambda i, ea, eb: (ea[i], 0, 0))),
                pl.BlockSpec((None, d_ff, D), idx(lambda i, ea, eb: (ea[i], 0, 0))),
                pl.BlockSpec((None, D, 2 * d_ff), idx(lambda i, ea, eb: (eb[i], 0, 0))),
                pl.BlockSpec((None, d_ff, D), idx(lambda i, ea, eb: (eb[i], 0, 0))),
            ],
            out_specs=pl.BlockSpec((blk, LANES), idx(lambda i, ea, eb: (i, 0))),
        ),
        out_shape=jax.ShapeDtypeStruct(xs.shape, f32),
        compiler_params=pltpu.CompilerParams(
            dimension_semantics=("arbitrary",), vmem_limit_bytes=VMEM_LIMIT),
    )(blk_ea, blk_eb, blk_valid, xs, g2, gf, wr, wgu, wd, wgu, wd)


def _moe(x1, cls_t, counts, g2, gf, wr, wgu, wd, final_norm, split=None):
    T, D = x1.shape
    n_rows = T + N_CLASSES * MOE_BLOCK
    pad_starts, blk_ea, blk_eb, blk_valid = _block_plan(counts[0, :N_CLASSES], n_rows // MOE_BLOCK)
    pos3 = _positions(cls_t, pad_starts).reshape(T // TOKEN_TILE, 1, TOKEN_TILE)
    xs = _dispatch(pos3, x1, n_rows)
    ys = _experts(xs, blk_ea, blk_eb, blk_valid, g2, gf, wr, wgu, wd, final_norm)
    return _combine(pos3, ys, D, split)


def kernel(x_prompt, x_sample, norm1_g, w_in, lambda_q1, lambda_k1, lambda_q2, lambda_k2, subln_g,
           w_out, norm2_g, w_router_group, w_router_expert, w_gate, w_up, w_down, final_norm_g):
    n_prompt = x_prompt.shape[0]
    x = jnp.concatenate([x_prompt, x_sample], axis=0)
    nb, seq, D = x.shape
    depth = w_in.shape[0]
    x2d = x.reshape(nb * seq, D)
    cos_np, sin_np = _rope_tables(seq)
    cos, sin = jnp.asarray(cos_np), jnp.asarray(sin_np)
    row = lambda v: v.reshape(1, -1).astype(f32)

    for l in range(depth):
        qa, ka, va, qb, kb, vb = _inproj(x2d, row(norm1_g[l]), w_in[l].astype(bf16), cos, sin, seq)
        to_seq = lambda t: t.reshape(nb, seq, t.shape[-1])
        oa = _mixer_a(to_seq(qa), to_seq(ka), to_seq(va))
        lam_init = 0.8 - 0.6 * math.exp(-0.3 * l)
        lam = (jnp.exp(jnp.sum(lambda_q1[l].astype(f32) * lambda_k1[l].astype(f32)))
               - jnp.exp(jnp.sum(lambda_q2[l].astype(f32) * lambda_k2[l].astype(f32)))
               + lam_init).reshape(1)
        ob = _mixer_b(lam, to_seq(qb), to_seq(kb), to_seq(vb), row(subln_g[l]), 1.0 - lam_init)

        w_router = jnp.concatenate([w_router_group[l], w_router_expert[l]], axis=1).astype(f32)
        w_router = jnp.pad(w_router, ((0, 0), (0, LANES - w_router.shape[1])))
        wr_hi = w_router.astype(bf16)
        wr_lo = (w_router - wr_hi.astype(f32)).astype(bf16)
        wo = w_out[l].astype(bf16)
        x1, cls_t, counts = _outproj(oa.reshape(nb * seq, WIDTH_A), ob.reshape(nb * seq, WIDTH_B),
                                     x2d, wo[:WIDTH_A], wo[WIDTH_A:], row(norm2_g[l]), wr_hi, wr_lo)

        wgu = jnp.concatenate([w_gate[l], w_up[l]], axis=-1).astype(bf16)
        last = l == depth - 1
        x2d = _moe(x1, cls_t, counts, row(norm2_g[l]), row(final_norm_g), wr_hi, wgu,
                   w_down[l].astype(bf16), final_norm=last,
                   split=n_prompt * seq // TOKEN_TILE if last else None)

    y_prompt, y_sample = x2d
    return y_prompt.reshape(n_prompt, seq, D), y_sample.reshape(nb - n_prompt, seq, D)
```

```python
import functools
import math

import numpy as np
import jax
import jax.numpy as jnp
from jax import lax
from jax.experimental import pallas as pl
from jax.experimental.pallas import tpu as pltpu

HEAD_DIM = 64
N_HEADS_A = 8
WIDTH_A = N_HEADS_A * HEAD_DIM
DIL_PATTERNS = ((128, 1), (512, 4), (2048, 16))
N_HEADS_B = 4
QK_WIDTH_B = N_HEADS_B * 2 * HEAD_DIM
V_DIM_B = 2 * HEAD_DIM
WIDTH_B = N_HEADS_B * V_DIM_B
IN_COLS = 3 * WIDTH_A + 2 * QK_WIDTH_B + WIDTH_B
ROPE_THETA = 10000.0
EPS = 1e-6
N_GROUPS = 4
EXPERTS_PER_GROUP = 8
N_EXPERTS = N_GROUPS * EXPERTS_PER_GROUP
PAIRS_PER_GROUP = EXPERTS_PER_GROUP * (EXPERTS_PER_GROUP - 1) // 2
N_CLASSES = N_GROUPS * PAIRS_PER_GROUP
NEG = -1e30

LANES = 128
SUBLANES = 8
TOKEN_TILE = 512
Q_TILE_B = 256
KEY_CHUNK_B = 1024
BAND_BLOCK = 128
BAND_RADIUS = 64
BAND_WINDOW = BAND_BLOCK + 2 * BAND_RADIUS
BAND_UNROLL = 16
MOE_BLOCK = 256
DMA_UNROLL = 8
VMEM_LIMIT = 56 * 1024 * 1024

QK_SCALE = HEAD_DIM ** -0.5 * math.log2(math.e)

f32 = jnp.float32
bf16 = jnp.bfloat16


def _rope_tables(seq):
    half = HEAD_DIM // 2
    inv = ROPE_THETA ** (-np.arange(half, dtype=np.float64) / half)
    lane = np.arange(LANES)
    ang = np.arange(seq, dtype=np.float64)[:, None] * inv[lane % half][None, :]
    sign = np.where((lane % HEAD_DIM) < half, -1.0, 1.0)[None, :]
    return np.cos(ang).astype(np.float32), (np.sin(ang) * sign).astype(np.float32)


def _row_source_specs(sources, width):
    if len(sources) == 1:
        return [pl.BlockSpec((TOKEN_TILE, width), lambda i: (i, 0))]
    n_first = sources[0].shape[0] // TOKEN_TILE
    return [pl.BlockSpec((TOKEN_TILE, width), lambda i: (jnp.minimum(i, n_first - 1), 0)),
            pl.BlockSpec((TOKEN_TILE, width), lambda i: (jnp.maximum(i - n_first, 0), 0))]


def _read_row_source(refs, n_first):
    if len(refs) == 1:
        return refs[0][...]
    return jnp.where(pl.program_id(0) < n_first, refs[0][...], refs[1][...])


def _inproj_kernel(*refs, n_first):
    (g_ref, w_ref, cos_ref, sin_ref, qa_ref, ka_ref, va_ref, qb_ref, kb_ref, vb_ref) = refs[-10:]
    x = _read_row_source(refs[:-10], n_first)
    ms = jnp.mean(x * x, axis=-1, keepdims=True)
    h = (x * lax.rsqrt(ms + EPS) * g_ref[...]).astype(bf16)
    p = jnp.dot(h, w_ref[...], preferred_element_type=f32)
    cos = cos_ref[...]
    sin = sin_ref[...]
    lane = lax.broadcasted_iota(jnp.int32, cos.shape, 1)
    first_half = (lane & (HEAD_DIM - 1)) < (HEAD_DIM // 2)

    def rope(t):
        rot = jnp.where(first_half, pltpu.roll(t, LANES - HEAD_DIM // 2, 1),
                        pltpu.roll(t, HEAD_DIM // 2, 1))
        return t * cos + rot * sin

    plan = ((qa_ref, 0, True, QK_SCALE), (ka_ref, WIDTH_A, True, None),
            (va_ref, 2 * WIDTH_A, False, None),
            (qb_ref, 3 * WIDTH_A, True, QK_SCALE), (kb_ref, 3 * WIDTH_A + QK_WIDTH_B, True, None),
            (vb_ref, 3 * WIDTH_A + 2 * QK_WIDTH_B, False, None))
    for ref, off, rotary, scale in plan:
        for c in range(ref.shape[1] // LANES):
            t = p[:, off + c * LANES: off + (c + 1) * LANES]
            if rotary:
                t = rope(t)
            if scale is not None:
                t = t * scale
            ref[:, c * LANES:(c + 1) * LANES] = t.astype(bf16)


def _inproj(x_sources, g, w, cos, sin, seq):
    T = sum(x.shape[0] for x in x_sources)
    D = x_sources[0].shape[1]
    tiles_per_seq = seq // TOKEN_TILE
    out = jax.ShapeDtypeStruct((T, WIDTH_A), bf16)
    row_spec = lambda width: pl.BlockSpec((TOKEN_TILE, width), lambda i: (i, 0))
    return pl.pallas_call(
        functools.partial(_inproj_kernel, n_first=x_sources[0].shape[0] // TOKEN_TILE),
        grid=(T // TOKEN_TILE,),
        in_specs=_row_source_specs(x_sources, D) + [
            pl.BlockSpec((1, D), lambda i: (0, 0)),
            pl.BlockSpec((D, IN_COLS), lambda i: (0, 0)),
            pl.BlockSpec((TOKEN_TILE, LANES), lambda i: (i % tiles_per_seq, 0)),
            pl.BlockSpec((TOKEN_TILE, LANES), lambda i: (i % tiles_per_seq, 0)),
        ],
        out_specs=[row_spec(WIDTH_A)] * 6,
        out_shape=[out] * 6,
        compiler_params=pltpu.CompilerParams(
            dimension_semantics=("arbitrary",), vmem_limit_bytes=VMEM_LIMIT),
    )(*x_sources, g, w, cos, sin)


def _band_block(q, k, v, bias):
    nq = q.shape[0]
    lane = lax.broadcasted_iota(jnp.int32, (nq, LANES), 1)
    head0 = lane < HEAD_DIM
    parts = []
    for sel in (head0, jnp.logical_not(head0)):
        qm = jnp.where(sel, q, jnp.zeros_like(q))
        s = lax.dot_general(qm, k, (((1,), (1,)), ((), ())), preferred_element_type=f32) + bias
        m = jnp.max(s, axis=1, keepdims=True)
        e = jnp.exp2(s - m)
        l = jnp.sum(e, axis=1, keepdims=True)
        o = jnp.dot(e.astype(bf16), v, preferred_element_type=f32)
        parts.append((o, m, l))
    (o0, m0, l0), (o1, m1, l1) = parts
    return (jnp.where(head0, o0, o1),
            jnp.where(head0, jnp.broadcast_to(m0, (nq, LANES)), jnp.broadcast_to(m1, (nq, LANES))),
            jnp.where(head0, jnp.broadcast_to(l0, (nq, LANES)), jnp.broadcast_to(l1, (nq, LANES))))


def _mixer_a_kernel(q_ref, k_ref, v_ref, o_ref, qf, kf, vf, q2, k2, v2, qd, kd, vd, oc, mc, lc,
                    acc_s, m_s, l_s, bias_s):
    seq = q_ref.shape[0]
    n_blocks = seq // BAND_BLOCK

    rel = (lax.broadcasted_iota(jnp.int32, (BAND_BLOCK, BAND_WINDOW), 0)
           - lax.broadcasted_iota(jnp.int32, (BAND_BLOCK, BAND_WINDOW), 1))
    for j in range(3):
        bias_s[j] = jnp.where(jnp.abs(rel + j * BAND_RADIUS) <= BAND_RADIUS, 0.0, NEG)

    dils = [d for _, d in DIL_PATTERNS]

    def refine(i):
        ratio = dils[i] // dils[i - 1]
        len_prev, len_cur = seq // dils[i - 1], seq // dils[i]
        return [(pl.ds(s * len_prev + r, len_cur, stride=ratio),
                 pl.ds((s * ratio + r) * len_cur, len_cur))
                for s in range(dils[i - 1]) for r in range(ratio)]

    f32_bufs = ((qf, kf, vf), (q2, k2, v2))
    state = (acc_s, m_s, l_s)
    for i, dil in enumerate(dils):
        cls_len = seq // dil
        blocks_per_cls = cls_len // BAND_BLOCK
        if i == 0:
            src, dst = (q_ref, k_ref, v_ref), state
        else:
            prev_f32, cur_f32 = f32_bufs[(i - 1) % 2], f32_bufs[i % 2]
            if i == 1:
                for buf, ref in zip(prev_f32, (q_ref, k_ref, v_ref)):
                    buf[...] = ref[...].astype(f32)
            for strided, rows in refine(i):
                for major, cur, prev in zip((qd, kd, vd), cur_f32, prev_f32):
                    t = prev[strided, :]
                    major[rows, :] = t.astype(bf16)
                    if i + 1 < len(dils):
                        cur[rows, :] = t
            src, dst = (qd, kd, vd), (oc, mc, lc)

        def body(it, carry, src=src, dst=dst, cls_len=cls_len, blocks_per_cls=blocks_per_cls):
            for u in range(BAND_UNROLL):
                b = it * BAND_UNROLL + u
                blk = b % blocks_per_cls
                base = (b // blocks_per_cls) * cls_len
                q0 = blk * BAND_BLOCK
                ws = jnp.clip(q0 - BAND_RADIUS, 0, cls_len - BAND_WINDOW)
                q_rows = pl.ds(pl.multiple_of(base + q0, BAND_BLOCK), BAND_BLOCK)
                k_rows = pl.ds(pl.multiple_of(base + ws, BAND_RADIUS), BAND_WINDOW)
                o, m, l = _band_block(src[0][q_rows, :], src[1][k_rows, :], src[2][k_rows, :],
                                      bias_s[(q0 - ws) // BAND_RADIUS])
                dst[0][q_rows, :] = o
                dst[1][q_rows, :] = m
                dst[2][q_rows, :] = l
            return carry

        lax.fori_loop(0, n_blocks // BAND_UNROLL, body, 0)

        if i > 0:
            new_state = f32_bufs[(i - 1) % 2]
            for strided, rows in refine(i):
                m_old = state[1][strided, :]
                m_pat = mc[rows, :]
                m_new = jnp.maximum(m_old, m_pat)
                a_old = jnp.exp2(m_old - m_new)
                a_pat = jnp.exp2(m_pat - m_new)
                new_state[0][rows, :] = state[0][strided, :] * a_old + oc[rows, :] * a_pat
                new_state[2][rows, :] = state[2][strided, :] * a_old + lc[rows, :] * a_pat
                new_state[1][rows, :] = m_new
            state = new_state

    hops = (acc_s, m_s, l_s)
    assert len(dils) == len(hops) and state[0] is not acc_s
    hops[0][...] = state[0][...] / state[2][...]
    for hop, i in enumerate(reversed(range(1, len(dils)))):
        for strided, rows in refine(i):
            hops[hop + 1][strided, :] = hops[hop][rows, :]
    o_ref[...] = hops[len(dils) - 1][...].astype(o_ref.dtype)


def _mixer_a(q, k, v):
    nb, seq, width = q.shape
    spec = pl.BlockSpec((None, seq, LANES), lambda b, p: (b, 0, p))
    tile_f32 = pltpu.VMEM((seq, LANES), f32)
    tile_bf16 = pltpu.VMEM((seq, LANES), bf16)
    return pl.pallas_call(
        _mixer_a_kernel,
        grid=(nb, width // LANES),
        in_specs=[spec, spec, spec],
        out_specs=spec,
        out_shape=jax.ShapeDtypeStruct((nb, seq, width), bf16),
        scratch_shapes=[tile_f32] * 6 + [tile_bf16] * 3 + [tile_f32] * 6
        + [pltpu.VMEM((3, BAND_BLOCK, BAND_WINDOW), f32)],
        compiler_params=pltpu.CompilerParams(
            dimension_semantics=("arbitrary", "arbitrary"), vmem_limit_bytes=VMEM_LIMIT),
    )(q, k, v)


PIPELINE_DEPTH_B = 1


def _mixer_b_kernel(lam_ref, q_ref, k_ref, v_ref, g_ref, o_ref, s_even, s_odd, m_fin, *, out_scale):
    step = pl.program_id(0)
    n_q, seq = s_even.shape[1:]
    kc = KEY_CHUNK_B

    @pl.when(step == 0)
    def _():
        for ref in (s_even, s_odd, m_fin):
            ref[...] = jnp.zeros_like(ref)

    def stages(s_w, s_r):
        q = q_ref[...]
        lane = lax.broadcasted_iota(jnp.int32, q.shape, 1)
        first = lane < HEAD_DIM
        zero = jnp.zeros_like(q)
        q_sub = (jnp.where(first, q, zero), jnp.where(first, zero, q))
        nt = (((1,), (1,)), ((), ()))
        m_prev = (m_fin[0], m_fin[1])
        row_max = [None, None]
        row_sum = [None, None]
        acc = [None, None]
        for c in range(seq // kc):
            keys = slice(c * kc, (c + 1) * kc)
            k_c = k_ref[keys, :]
            v_c = v_ref[keys, :]
            for sub in range(2):
                s = lax.dot_general(q_sub[sub], k_c, nt, preferred_element_type=f32)
                s_w[sub, :, keys] = s
                part = functools.reduce(
                    jnp.maximum, [s[:, j * LANES:(j + 1) * LANES] for j in range(kc // LANES)])
                row_max[sub] = part if c == 0 else jnp.maximum(row_max[sub], part)
            for sub in range(2):
                chunk_sum = None
                pieces = []
                for j in range(kc // LANES):
                    cols = slice(c * kc + j * LANES, c * kc + (j + 1) * LANES)
                    e = jnp.exp2(s_r[sub, :, cols] - m_prev[sub])
                    chunk_sum = e if chunk_sum is None else chunk_sum + e
                    pieces.append(e.astype(bf16))
                row_sum[sub] = chunk_sum if c == 0 else row_sum[sub] + chunk_sum
                pv = jnp.dot(jnp.concatenate(pieces, axis=1), v_c, preferred_element_type=f32)
                acc[sub] = pv if c == 0 else acc[sub] + pv

        for sub in range(2):
            m_fin[sub] = jnp.broadcast_to(jnp.max(row_max[sub], axis=1, keepdims=True), (n_q, LANES))
        c0 = 1.0 / jnp.sum(row_sum[0], axis=1, keepdims=True)
        c1 = lam_ref[0] / jnp.sum(row_sum[1], axis=1, keepdims=True)
        o = acc[0] * c0 - acc[1] * c1
        ms = jnp.mean(o * o, axis=-1, keepdims=True)
        o_ref[...] = (o * lax.rsqrt(ms + EPS) * g_ref[...] * out_scale).astype(o_ref.dtype)

    @pl.when(step % 2 == 0)
    def _():
        stages(s_even, s_odd)

    @pl.when(step % 2 == 1)
    def _():
        stages(s_odd, s_even)


def _mixer_b(lam, q, k, v, g, out_scale):
    nb, seq, width = q.shape
    heads = width // LANES
    q_tiles = seq // Q_TILE_B
    n_tiles = nb * heads * q_tiles

    def tile(t):
        return t // (heads * q_tiles), (t // q_tiles) % heads, t % q_tiles

    def score_tile(step):
        return tile(jnp.minimum(step, n_tiles - 1))

    def value_tile(step):
        return tile(jnp.clip(step - PIPELINE_DEPTH_B, 0, n_tiles - 1))

    def q_map(step, lam):
        b, h, i = score_tile(step)
        return b, i, h

    def k_map(step, lam):
        b, h, _ = score_tile(step)
        return b, 0, h

    def v_map(step, lam):
        b, h, _ = value_tile(step)
        return b, 0, h

    def o_map(step, lam):
        b, h, i = value_tile(step)
        return b, i, h

    return pl.pallas_call(
        functools.partial(_mixer_b_kernel, out_scale=out_scale),
        grid_spec=pltpu.PrefetchScalarGridSpec(
            num_scalar_prefetch=1,
            grid=(n_tiles + PIPELINE_DEPTH_B,),
            in_specs=[pl.BlockSpec((None, Q_TILE_B, LANES), q_map),
                      pl.BlockSpec((None, seq, LANES), k_map),
                      pl.BlockSpec((None, seq, LANES), v_map),
                      pl.BlockSpec((1, LANES), lambda step, lam: (0, 0))],
            out_specs=pl.BlockSpec((None, Q_TILE_B, LANES), o_map),
            scratch_shapes=[pltpu.VMEM((2, Q_TILE_B, seq), f32)] * 2
            + [pltpu.VMEM((2, Q_TILE_B, LANES), f32)],
        ),
        out_shape=jax.ShapeDtypeStruct((nb, seq, width), bf16),
        compiler_params=pltpu.CompilerParams(
            dimension_semantics=("arbitrary",), vmem_limit_bytes=VMEM_LIMIT),
    )(lam, q, k, v, g)


def _outproj_kernel(*refs, n_first):
    (oa_ref, ob_ref, wa_ref, wb_ref, g_ref, wr_ref, x1_ref, cls_t_ref, counts_ref) = refs[-9:]
    n = oa_ref.shape[0]
    x1 = (_read_row_source(refs[:-9], n_first)
          + jnp.dot(oa_ref[...], wa_ref[...], preferred_element_type=f32)
          + jnp.dot(ob_ref[...], wb_ref[...], preferred_element_type=f32))
    x1_ref[...] = x1
    ms = jnp.mean(x1 * x1, axis=-1, keepdims=True)
    h = x1 * lax.rsqrt(ms + EPS) * g_ref[...]
    h_hi = h.astype(bf16)
    h_lo = (h - h_hi.astype(f32)).astype(bf16)
    parts = jnp.dot(jnp.concatenate([h_hi, h_lo], axis=0), wr_ref[...], preferred_element_type=f32)
    logits = (parts[:n, :LANES] + parts[:n, LANES:]) + (parts[n:, :LANES] + parts[n:, LANES:])

    lane = lax.broadcasted_iota(jnp.int32, logits.shape, 1)
    lane_f = lane.astype(f32)

    def first_argmax(vals):
        top = jnp.max(vals, axis=1, keepdims=True)
        first = jnp.min(jnp.where(vals == top, lane_f, float(LANES)), axis=1, keepdims=True)
        return first.astype(jnp.int32)

    g_idx = first_argmax(jnp.where(lane < N_GROUPS, logits, NEG))
    group_first = N_GROUPS + g_idx * EXPERTS_PER_GROUP
    in_group = (lane >= group_first) & (lane < group_first + EXPERTS_PER_GROUP)
    e_logits = jnp.where(in_group, logits, NEG)
    i1 = first_argmax(e_logits)
    i2 = first_argmax(jnp.where(lane == i1, NEG, e_logits))
    lo = jnp.minimum(i1, i2) - group_first
    hi = jnp.maximum(i1, i2) - group_first
    pair = jnp.right_shift(lo * (2 * EXPERTS_PER_GROUP - 1 - lo), 1) + (hi - lo - 1)
    cls = g_idx * PAIRS_PER_GROUP + pair
    cls_lanes = jnp.broadcast_to(cls.astype(f32), logits.shape)
    cls_t_ref[...] = jnp.transpose(cls_lanes)[:SUBLANES, :]

    @pl.when(pl.program_id(0) == 0)
    def _():
        counts_ref[...] = jnp.zeros_like(counts_ref)

    counts_ref[...] += jnp.sum(jnp.where(lane == cls, 1.0, 0.0), axis=0, keepdims=True)


def _outproj(oa, ob, x_sources, wo, g, wr):
    T, D = oa.shape[0], wo.shape[1]
    row_spec = lambda width: pl.BlockSpec((TOKEN_TILE, width), lambda i: (i, 0))
    full = lambda a: pl.BlockSpec(a.shape, lambda i: (0, 0))
    assert WIDTH_A == WIDTH_B
    return pl.pallas_call(
        functools.partial(_outproj_kernel, n_first=x_sources[0].shape[0] // TOKEN_TILE),
        grid=(T // TOKEN_TILE,),
        in_specs=_row_source_specs(x_sources, D)
        + [row_spec(WIDTH_A), row_spec(WIDTH_B),
           pl.BlockSpec((WIDTH_A, D), lambda i: (0, 0)), pl.BlockSpec((WIDTH_B, D), lambda i: (1, 0)),
           full(g), full(wr)],
        out_specs=[row_spec(D),
                   pl.BlockSpec((SUBLANES, TOKEN_TILE), lambda i: (0, i)),
                   pl.BlockSpec((1, LANES), lambda i: (0, 0))],
        out_shape=[jax.ShapeDtypeStruct((T, D), f32),
                   jax.ShapeDtypeStruct((SUBLANES, T), f32),
                   jax.ShapeDtypeStruct((1, LANES), f32)],
        compiler_params=pltpu.CompilerParams(
            dimension_semantics=("arbitrary",), vmem_limit_bytes=VMEM_LIMIT),
    )(*x_sources, oa, ob, wo, wo, g, wr)


def _pair_tables():
    lo, hi = [], []
    for g in range(N_GROUPS):
        for a in range(EXPERTS_PER_GROUP):
            for b in range(a + 1, EXPERTS_PER_GROUP):
                lo.append(g * EXPERTS_PER_GROUP + a)
                hi.append(g * EXPERTS_PER_GROUP + b)
    return np.asarray(lo, np.int32), np.asarray(hi, np.int32)


def _block_plan(counts, n_blocks):
    counts = counts.astype(jnp.int32)
    padded = (counts + MOE_BLOCK - 1) // MOE_BLOCK * MOE_BLOCK
    pad_ends = jnp.cumsum(padded)
    pad_starts = pad_ends - padded
    blk_start = jnp.arange(n_blocks, dtype=jnp.int32) * MOE_BLOCK
    blk_cls = jnp.minimum(jnp.sum(pad_ends[None, :] <= blk_start[:, None], axis=1),
                          N_CLASSES - 1).astype(jnp.int32)
    onehot = (blk_cls[:, None] == jnp.arange(N_CLASSES, dtype=jnp.int32)[None, :]).astype(jnp.int32)
    used = jnp.sum(onehot * (counts - (blk_start[:, None] - pad_starts[None, :])), axis=1)
    blk_valid = jnp.where(blk_start < pad_ends[-1], jnp.clip(used, 0, MOE_BLOCK), 0).astype(jnp.int32)
    lo_tab, hi_tab = _pair_tables()
    blk_ea = jnp.sum(onehot * jnp.asarray(lo_tab)[None, :], axis=1).astype(jnp.int32)
    blk_eb = jnp.sum(onehot * jnp.asarray(hi_tab)[None, :], axis=1).astype(jnp.int32)
    return pad_starts, blk_ea, blk_eb, blk_valid


def _positions_kernel(cls_t_ref, start_ref, pos_ref, base_s):
    n = cls_t_ref.shape[1]

    @pl.when(pl.program_id(0) == 0)
    def _():
        base_s[...] = start_ref[...]

    cls = cls_t_ref[0:1, :].astype(jnp.int32)
    onehot = lax.broadcasted_iota(jnp.int32, (LANES, n), 0) == cls
    earlier = (lax.broadcasted_iota(jnp.int32, (n, n), 0)
               < lax.broadcasted_iota(jnp.int32, (n, n), 1))
    onehot_f = jnp.where(onehot, 1.0, 0.0)
    rank = jnp.dot(onehot_f.astype(bf16), jnp.where(earlier, 1.0, 0.0).astype(bf16),
                   preferred_element_type=f32)
    base = base_s[...]
    pos = jnp.sum(jnp.where(onehot, rank + base, 0.0), axis=0, keepdims=True)
    pos_ref[...] = pos.astype(jnp.int32)
    base_s[...] = base + jnp.sum(onehot_f, axis=1, keepdims=True)


def _positions(cls_t, pad_starts):
    T = cls_t.shape[1]
    start_col = jnp.zeros((LANES, 1), f32).at[:N_CLASSES, 0].set(pad_starts.astype(f32))
    return pl.pallas_call(
        _positions_kernel,
        grid=(T // TOKEN_TILE,),
        in_specs=[pl.BlockSpec((SUBLANES, TOKEN_TILE), lambda i: (0, i)),
                  pl.BlockSpec((LANES, 1), lambda i: (0, 0))],
        out_specs=pl.BlockSpec((1, TOKEN_TILE), lambda i: (0, i)),
        out_shape=jax.ShapeDtypeStruct((1, T), jnp.int32),
        scratch_shapes=[pltpu.VMEM((LANES, 1), f32)],
        compiler_params=pltpu.CompilerParams(
            dimension_semantics=("arbitrary",), vmem_limit_bytes=VMEM_LIMIT),
    )(cls_t, start_col)


def _token_copy_loops(make_copy, n):
    def start(r, c):
        make_copy(r).start()
        return c
    lax.fori_loop(0, n, start, 0, unroll=DMA_UNROLL)

    def wait(r, c):
        make_copy(r).wait()
        return c
    lax.fori_loop(0, n, wait, 0, unroll=DMA_UNROLL)


def _dispatch_kernel(pos_ref, x_ref, init_hbm, xs_hbm, stage, sem):
    del init_hbm
    n, D = x_ref.shape
    chunks = D // LANES
    for j in range(chunks):
        stage[pl.ds(j, n, stride=chunks), :] = x_ref[:, j * LANES:(j + 1) * LANES]

    def copy(r):
        src = stage.at[pl.ds(pl.multiple_of(r * chunks, chunks), chunks)]
        dst = xs_hbm.at[pl.ds(pl.multiple_of(pos_ref[0, r] * chunks, chunks), chunks)]
        return pltpu.make_async_copy(src, dst, sem)

    _token_copy_loops(copy, n)


def _dispatch(pos3, x1, n_rows):
    n_tiles, _, tile = pos3.shape
    D = x1.shape[1]
    chunks = D // LANES
    shape = jax.ShapeDtypeStruct((n_rows * chunks, LANES), f32)
    return pl.pallas_call(
        _dispatch_kernel,
        grid=(n_tiles,),
        in_specs=[pl.BlockSpec((None, 1, tile), lambda i: (i, 0, 0), memory_space=pltpu.SMEM),
                  pl.BlockSpec((tile, D), lambda i: (i, 0)),
                  pl.BlockSpec(memory_space=pl.ANY)],
        out_specs=pl.BlockSpec(memory_space=pl.ANY),
        out_shape=shape,
        scratch_shapes=[pltpu.VMEM((tile * chunks, LANES), f32), pltpu.SemaphoreType.DMA],
        input_output_aliases={2: 0},
        compiler_params=pltpu.CompilerParams(
            dimension_semantics=("arbitrary",), vmem_limit_bytes=VMEM_LIMIT),
    )(pos3, x1, jnp.zeros(shape.shape, f32))


def _combine_kernel(pos_ref, ys_hbm, *rest, split):
    outs, (stage, sem) = rest[:-2], rest[-2:]
    n, D = outs[0].shape
    chunks = D // LANES

    def copy(r):
        src = ys_hbm.at[pl.ds(pl.multiple_of(pos_ref[0, r] * chunks, chunks), chunks)]
        dst = stage.at[pl.ds(pl.multiple_of(r * chunks, chunks), chunks)]
        return pltpu.make_async_copy(src, dst, sem)

    _token_copy_loops(copy, n)

    def write(o_ref):
        for j in range(chunks):
            o_ref[:, j * LANES:(j + 1) * LANES] = stage[pl.ds(j, n, stride=chunks), :]

    if split is None:
        write(outs[0])
    else:
        pl.when(pl.program_id(0) < split)(lambda: write(outs[0]))
        pl.when(pl.program_id(0) >= split)(lambda: write(outs[1]))


def _combine(pos3, ys, D, split=None):
    n_tiles, _, tile = pos3.shape
    chunks = D // LANES
    if split is None:
        out_specs = pl.BlockSpec((tile, D), lambda i: (i, 0))
        out_shape = jax.ShapeDtypeStruct((n_tiles * tile, D), f32)
    else:
        out_specs = [pl.BlockSpec((tile, D), lambda i: (jnp.minimum(i, split - 1), 0)),
                     pl.BlockSpec((tile, D), lambda i: (jnp.maximum(i - split, 0), 0))]
        out_shape = [jax.ShapeDtypeStruct((split * tile, D), f32),
                     jax.ShapeDtypeStruct(((n_tiles - split) * tile, D), f32)]
    return pl.pallas_call(
        functools.partial(_combine_kernel, split=split),
        grid=(n_tiles,),
        in_specs=[pl.BlockSpec((None, 1, tile), lambda i: (i, 0, 0), memory_space=pltpu.SMEM),
                  pl.BlockSpec(memory_space=pl.ANY)],
        out_specs=out_specs,
        out_shape=out_shape,
        scratch_shapes=[pltpu.VMEM((tile * chunks, LANES), f32), pltpu.SemaphoreType.DMA],
        compiler_params=pltpu.CompilerParams(
            dimension_semantics=("arbitrary",), vmem_limit_bytes=VMEM_LIMIT),
    )(pos3, ys)


def _experts_kernel(ea_ref, eb_ref, nv_ref, xs_ref, g2_ref, gf_ref, wr_ref,
                    wgu_a_ref, wd_a_ref, wgu_b_ref, wd_b_ref, ys_ref, *, final_norm):
    D = g2_ref.shape[1]
    chunks = D // LANES
    n = xs_ref.shape[0] // chunks
    d_ff = wd_a_ref.shape[0]
    i = pl.program_id(0)
    active = nv_ref[i] > 0

    @pl.when(active)
    def _():
        x = jnp.concatenate([xs_ref[pl.ds(j, n, stride=chunks), :] for j in range(chunks)], axis=1)
        ms = jnp.mean(x * x, axis=-1, keepdims=True)
        h = (x * lax.rsqrt(ms + EPS) * g2_ref[...]).astype(bf16)

        logits = jnp.dot(h, wr_ref[...], preferred_element_type=f32)
        lane = lax.broadcasted_iota(jnp.int32, logits.shape, 1)
        pick = lambda idx: jnp.sum(jnp.where(lane == idx, logits, 0.0), axis=1, keepdims=True)
        ea, eb = ea_ref[i], eb_ref[i]
        l_group = pick(ea // EXPERTS_PER_GROUP)
        p_group = 1.0 / jnp.sum(jnp.where(lane < N_GROUPS, jnp.exp(logits - l_group), 0.0),
                                axis=1, keepdims=True)
        l_a, l_b = pick(N_GROUPS + ea), pick(N_GROUPS + eb)
        gates = (p_group / (1.0 + jnp.exp(l_b - l_a)), p_group / (1.0 + jnp.exp(l_a - l_b)))

        y = x
        for gate, wgu_ref, wd_ref in ((gates[0], wgu_a_ref, wd_a_ref), (gates[1], wgu_b_ref, wd_b_ref)):
            gu = jnp.dot(h, wgu_ref[...], preferred_element_type=f32)
            a = gu[:, :d_ff]
            act = (a * jax.nn.sigmoid(a) * gu[:, d_ff:]).astype(bf16)
            y = y + gate * jnp.dot(act, wd_ref[...], preferred_element_type=f32)
        if final_norm:
            ms = jnp.mean(y * y, axis=-1, keepdims=True)
            y = y * lax.rsqrt(ms + EPS) * gf_ref[...]
        for j in range(chunks):
            ys_ref[pl.ds(j, n, stride=chunks), :] = y[:, j * LANES:(j + 1) * LANES]

    @pl.when(jnp.logical_not(active))
    def _():
        ys_ref[...] = jnp.zeros_like(ys_ref)


def _experts(xs, blk_ea, blk_eb, blk_valid, g2, gf, wr, wgu, wd, final_norm):
    D = g2.shape[1]
    chunks = D // LANES
    d_ff = wd.shape[1]
    blk = MOE_BLOCK * chunks
    idx = lambda f: (lambda i, ea, eb, nv: f(i, ea, eb))
    const = lambda a: pl.BlockSpec(a.shape, idx(lambda i, ea, eb: (0, 0)))
    return pl.pallas_call(
        functools.partial(_experts_kernel, final_norm=final_norm),
        grid_spec=pltpu.PrefetchScalarGridSpec(
            num_scalar_prefetch=3,
            grid=(xs.shape[0] // blk,),
            in_specs=[
                pl.BlockSpec((blk, LANES), idx(lambda i, ea, eb: (i, 0))),
                const(g2), const(gf), const(wr),
                pl.BlockSpec((None, D, 2 * d_ff), idx(lambda i, ea, eb: (ea[i], 0, 0))),
                pl.BlockSpec((None, d_ff, D), idx(lambda i, ea, eb: (ea[i], 0, 0))),
                pl.BlockSpec((None, D, 2 * d_ff), idx(lambda i, ea, eb: (eb[i], 0, 0))),
                pl.BlockSpec((None, d_ff, D), idx(lambda i, ea, eb: (eb[i], 0, 0))),
            ],
            out_specs=pl.BlockSpec((blk, LANES), idx(lambda i, ea, eb: (i, 0))),
        ),
        out_shape=jax.ShapeDtypeStruct(xs.shape, f32),
        compiler_params=pltpu.CompilerParams(
            dimension_semantics=("arbitrary",), vmem_limit_bytes=VMEM_LIMIT),
    )(blk_ea, blk_eb, blk_valid, xs, g2, gf, wr, wgu, wd, wgu, wd)


def _moe(x1, cls_t, counts, g2, gf, wr, wgu, wd, final_norm, split=None):
    T, D = x1.shape
    n_rows = T + N_CLASSES * MOE_BLOCK
    pad_starts, blk_ea, blk_eb, blk_valid = _block_plan(counts[0, :N_CLASSES], n_rows // MOE_BLOCK)
    pos3 = _positions(cls_t, pad_starts).reshape(T // TOKEN_TILE, 1, TOKEN_TILE)
    xs = _dispatch(pos3, x1, n_rows)
    ys = _experts(xs, blk_ea, blk_eb, blk_valid, g2, gf, wr, wgu, wd, final_norm)
    return _combine(pos3, ys, D, split)


def kernel(x_prompt, x_sample, norm1_g, w_in, lambda_q1, lambda_k1, lambda_q2, lambda_k2, subln_g,
           w_out, norm2_g, w_router_group, w_router_expert, w_gate, w_up, w_down, final_norm_g):
    n_prompt, seq, D = x_prompt.shape
    assert x_sample.shape[1:] == (seq, D)
    nb = n_prompt + x_sample.shape[0]
    depth = w_in.shape[0]
    x_sources = (x_prompt.reshape(n_prompt * seq, D), x_sample.reshape((nb - n_prompt) * seq, D))
    cos_np, sin_np = _rope_tables(seq)
    cos, sin = jnp.asarray(cos_np), jnp.asarray(sin_np)
    row = lambda v: v.reshape(1, -1).astype(f32)

    for l in range(depth):
        qa, ka, va, qb, kb, vb = _inproj(x_sources, row(norm1_g[l]), w_in[l].astype(bf16), cos, sin,
                                         seq)
        to_seq = lambda t: t.reshape(nb, seq, t.shape[-1])
        oa = _mixer_a(to_seq(qa), to_seq(ka), to_seq(va))
        lam_init = 0.8 - 0.6 * math.exp(-0.3 * l)
        lam = (jnp.exp(jnp.sum(lambda_q1[l].astype(f32) * lambda_k1[l].astype(f32)))
               - jnp.exp(jnp.sum(lambda_q2[l].astype(f32) * lambda_k2[l].astype(f32)))
               + lam_init).reshape(1)
        ob = _mixer_b(lam, to_seq(qb), to_seq(kb), to_seq(vb), row(subln_g[l]), 1.0 - lam_init)

        w_router = jnp.concatenate([w_router_group[l], w_router_expert[l]], axis=1).astype(f32)
        w_router = jnp.pad(w_router, ((0, 0), (0, LANES - w_router.shape[1])))
        wr_hi = w_router.astype(bf16)
        wr_lo = (w_router - wr_hi.astype(f32)).astype(bf16)
        x1, cls_t, counts = _outproj(oa.reshape(nb * seq, WIDTH_A), ob.reshape(nb * seq, WIDTH_B),
                                     x_sources, w_out[l].astype(bf16), row(norm2_g[l]),
                                     jnp.concatenate([wr_hi, wr_lo], axis=1))

        wgu = jnp.concatenate([w_gate[l], w_up[l]], axis=-1).astype(bf16)
        last = l == depth - 1
        y = _moe(x1, cls_t, counts, row(norm2_g[l]), row(final_norm_g), wr_hi, wgu,
                 w_down[l].astype(bf16), final_norm=last,
                 split=n_prompt * seq // TOKEN_TILE if last else None)
        x_sources = y if last else (y,)

    y_prompt, y_sample = x_sources
    return y_prompt.reshape(n_prompt, seq, D), y_sample.reshape(nb - n_prompt, seq, D)
```

```python
import functools
import math

import numpy as np
import jax
import jax.numpy as jnp
from jax import lax
from jax.experimental import pallas as pl
from jax.experimental.pallas import tpu as pltpu

HEAD_DIM = 64
N_HEADS_A = 8
WIDTH_A = N_HEADS_A * HEAD_DIM
DIL_PATTERNS = ((128, 1), (512, 4), (2048, 16))
N_HEADS_B = 4
QK_WIDTH_B = N_HEADS_B * 2 * HEAD_DIM
V_DIM_B = 2 * HEAD_DIM
WIDTH_B = N_HEADS_B * V_DIM_B
IN_COLS = 3 * WIDTH_A + 2 * QK_WIDTH_B + WIDTH_B
ROPE_THETA = 10000.0
EPS = 1e-6
N_GROUPS = 4
EXPERTS_PER_GROUP = 8
N_EXPERTS = N_GROUPS * EXPERTS_PER_GROUP
PAIRS_PER_GROUP = EXPERTS_PER_GROUP * (EXPERTS_PER_GROUP - 1) // 2
N_CLASSES = N_GROUPS * PAIRS_PER_GROUP
NEG = -1e30

LANES = 128
SUBLANES = 8
TOKEN_TILE = 512
Q_TILE_B = 256
KEY_CHUNK_B = 1024
BAND_BLOCK = 128
BAND_RADIUS = 64
BAND_WINDOW = BAND_BLOCK + 2 * BAND_RADIUS
BAND_UNROLL = 16
MOE_BLOCK = 256
DMA_UNROLL = 8
VMEM_LIMIT = 56 * 1024 * 1024

QK_SCALE = HEAD_DIM ** -0.5 * math.log2(math.e)

f32 = jnp.float32
bf16 = jnp.bfloat16


def _rope_tables(seq):
    half = HEAD_DIM // 2
    inv = ROPE_THETA ** (-np.arange(half, dtype=np.float64) / half)
    lane = np.arange(LANES)
    ang = np.arange(seq, dtype=np.float64)[:, None] * inv[lane % half][None, :]
    sign = np.where((lane % HEAD_DIM) < half, -1.0, 1.0)[None, :]
    return np.cos(ang).astype(np.float32), (np.sin(ang) * sign).astype(np.float32)


def _row_source_specs(sources, width):
    if len(sources) == 1:
        return [pl.BlockSpec((TOKEN_TILE, width), lambda i: (i, 0))]
    n_first = sources[0].shape[0] // TOKEN_TILE
    return [pl.BlockSpec((TOKEN_TILE, width), lambda i: (jnp.minimum(i, n_first - 1), 0)),
            pl.BlockSpec((TOKEN_TILE, width), lambda i: (jnp.maximum(i - n_first, 0), 0))]


def _read_row_source(refs, n_first):
    if len(refs) == 1:
        return refs[0][...]
    return jnp.where(pl.program_id(0) < n_first, refs[0][...], refs[1][...])


def _inproj_kernel(*refs, n_first):
    (g_ref, w_ref, cos_ref, sin_ref, qa_ref, ka_ref, va_ref, qb_ref, kb_ref, vb_ref) = refs[-10:]
    x = _read_row_source(refs[:-10], n_first)
    ms = jnp.mean(x * x, axis=-1, keepdims=True)
    h = (x * lax.rsqrt(ms + EPS) * g_ref[...]).astype(bf16)
    p = jnp.dot(h, w_ref[...], preferred_element_type=f32)
    cos = cos_ref[...]
    sin = sin_ref[...]
    lane = lax.broadcasted_iota(jnp.int32, cos.shape, 1)
    first_half = (lane & (HEAD_DIM - 1)) < (HEAD_DIM // 2)

    def rope(t):
        rot = jnp.where(first_half, pltpu.roll(t, LANES - HEAD_DIM // 2, 1),
                        pltpu.roll(t, HEAD_DIM // 2, 1))
        return t * cos + rot * sin

    plan = ((qa_ref, 0, True, QK_SCALE), (ka_ref, WIDTH_A, True, None),
            (va_ref, 2 * WIDTH_A, False, None),
            (qb_ref, 3 * WIDTH_A, True, QK_SCALE), (kb_ref, 3 * WIDTH_A + QK_WIDTH_B, True, None),
            (vb_ref, 3 * WIDTH_A + 2 * QK_WIDTH_B, False, None))
    for ref, off, rotary, scale in plan:
        for c in range(ref.shape[1] // LANES):
            t = p[:, off + c * LANES: off + (c + 1) * LANES]
            if rotary:
                t = rope(t)
            if scale is not None:
                t = t * scale
            ref[:, c * LANES:(c + 1) * LANES] = t.astype(bf16)


def _inproj(x_sources, g, w, cos, sin, seq):
    T = sum(x.shape[0] for x in x_sources)
    D = x_sources[0].shape[1]
    tiles_per_seq = seq // TOKEN_TILE
    out = jax.ShapeDtypeStruct((T, WIDTH_A), bf16)
    row_spec = lambda width: pl.BlockSpec((TOKEN_TILE, width), lambda i: (i, 0))
    return pl.pallas_call(
        functools.partial(_inproj_kernel, n_first=x_sources[0].shape[0] // TOKEN_TILE),
        grid=(T // TOKEN_TILE,),
        in_specs=_row_source_specs(x_sources, D) + [
            pl.BlockSpec((1, D), lambda i: (0, 0)),
            pl.BlockSpec((D, IN_COLS), lambda i: (0, 0)),
            pl.BlockSpec((TOKEN_TILE, LANES), lambda i: (i % tiles_per_seq, 0)),
            pl.BlockSpec((TOKEN_TILE, LANES), lambda i: (i % tiles_per_seq, 0)),
        ],
        out_specs=[row_spec(WIDTH_A)] * 6,
        out_shape=[out] * 6,
        compiler_params=pltpu.CompilerParams(
            dimension_semantics=("arbitrary",), vmem_limit_bytes=VMEM_LIMIT),
    )(*x_sources, g, w, cos, sin)


def _band_block(q, k, v, bias):
    nq = q.shape[0]
    lane = lax.broadcasted_iota(jnp.int32, (nq, LANES), 1)
    head0 = lane < HEAD_DIM
    parts = []
    for sel in (head0, jnp.logical_not(head0)):
        qm = jnp.where(sel, q, jnp.zeros_like(q))
        s = lax.dot_general(qm, k, (((1,), (1,)), ((), ())), preferred_element_type=f32) + bias
        m = jnp.max(s, axis=1, keepdims=True)
        e = jnp.exp2(s - m)
        l = jnp.sum(e, axis=1, keepdims=True)
        o = jnp.dot(e.astype(bf16), v, preferred_element_type=f32)
        parts.append((o, m, l))
    (o0, m0, l0), (o1, m1, l1) = parts
    return (jnp.where(head0, o0, o1),
            jnp.where(head0, jnp.broadcast_to(m0, (nq, LANES)), jnp.broadcast_to(m1, (nq, LANES))),
            jnp.where(head0, jnp.broadcast_to(l0, (nq, LANES)), jnp.broadcast_to(l1, (nq, LANES))))


def _mixer_a_kernel(q_ref, k_ref, v_ref, o_ref, qf, kf, vf, q2, k2, v2, qd, kd, vd, oc, mc, lc,
                    acc_s, m_s, l_s, bias_s):
    seq = q_ref.shape[0]
    n_blocks = seq // BAND_BLOCK

    rel = (lax.broadcasted_iota(jnp.int32, (BAND_BLOCK, BAND_WINDOW), 0)
           - lax.broadcasted_iota(jnp.int32, (BAND_BLOCK, BAND_WINDOW), 1))
    for j in range(3):
        bias_s[j] = jnp.where(jnp.abs(rel + j * BAND_RADIUS) <= BAND_RADIUS, 0.0, NEG)

    dils = [d for _, d in DIL_PATTERNS]

    def refine(i):
        ratio = dils[i] // dils[i - 1]
        len_prev, len_cur = seq // dils[i - 1], seq // dils[i]
        return [(pl.ds(s * len_prev + r, len_cur, stride=ratio),
                 pl.ds((s * ratio + r) * len_cur, len_cur))
                for s in range(dils[i - 1]) for r in range(ratio)]

    f32_bufs = ((qf, kf, vf), (q2, k2, v2))
    state = (acc_s, m_s, l_s)
    for i, dil in enumerate(dils):
        cls_len = seq // dil
        blocks_per_cls = cls_len // BAND_BLOCK
        if i == 0:
            src, dst = (q_ref, k_ref, v_ref), state
        else:
            prev_f32, cur_f32 = f32_bufs[(i - 1) % 2], f32_bufs[i % 2]
            if i == 1:
                for buf, ref in zip(prev_f32, (q_ref, k_ref, v_ref)):
                    buf[...] = ref[...].astype(f32)
            for strided, rows in refine(i):
                for major, cur, prev in zip((qd, kd, vd), cur_f32, prev_f32):
                    t = prev[strided, :]
                    major[rows, :] = t.astype(bf16)
                    if i + 1 < len(dils):
                        cur[rows, :] = t
            src, dst = (qd, kd, vd), (oc, mc, lc)

        def body(it, carry, src=src, dst=dst, cls_len=cls_len, blocks_per_cls=blocks_per_cls):
            for u in range(BAND_UNROLL):
                b = it * BAND_UNROLL + u
                blk = b % blocks_per_cls
                base = (b // blocks_per_cls) * cls_len
                q0 = blk * BAND_BLOCK
                ws = jnp.clip(q0 - BAND_RADIUS, 0, cls_len - BAND_WINDOW)
                q_rows = pl.ds(pl.multiple_of(base + q0, BAND_BLOCK), BAND_BLOCK)
                k_rows = pl.ds(pl.multiple_of(base + ws, BAND_RADIUS), BAND_WINDOW)
                o, m, l = _band_block(src[0][q_rows, :], src[1][k_rows, :], src[2][k_rows, :],
                                      bias_s[(q0 - ws) // BAND_RADIUS])
                dst[0][q_rows, :] = o
                dst[1][q_rows, :] = m
                dst[2][q_rows, :] = l
            return carry

        lax.fori_loop(0, n_blocks // BAND_UNROLL, body, 0)

        if i > 0:
            new_state = f32_bufs[(i - 1) % 2]
            for strided, rows in refine(i):
                m_old = state[1][strided, :]
                m_pat = mc[rows, :]
                m_new = jnp.maximum(m_old, m_pat)
                a_old = jnp.exp2(m_old - m_new)
                a_pat = jnp.exp2(m_pat - m_new)
                new_state[0][rows, :] = state[0][strided, :] * a_old + oc[rows, :] * a_pat
                new_state[2][rows, :] = state[2][strided, :] * a_old + lc[rows, :] * a_pat
                new_state[1][rows, :] = m_new
            state = new_state

    hops = (acc_s, m_s, l_s)
    assert len(dils) == len(hops) and state[0] is not acc_s
    hops[0][...] = state[0][...] / state[2][...]
    for hop, i in enumerate(reversed(range(1, len(dils)))):
        for strided, rows in refine(i):
            hops[hop + 1][strided, :] = hops[hop][rows, :]
    o_ref[...] = hops[len(dils) - 1][...].astype(o_ref.dtype)


def _mixer_a(q, k, v):
    nb, seq, width = q.shape
    spec = pl.BlockSpec((None, seq, LANES), lambda b, p: (b, 0, p))
    tile_f32 = pltpu.VMEM((seq, LANES), f32)
    tile_bf16 = pltpu.VMEM((seq, LANES), bf16)
    return pl.pallas_call(
        _mixer_a_kernel,
        grid=(nb, width // LANES),
        in_specs=[spec, spec, spec],
        out_specs=spec,
        out_shape=jax.ShapeDtypeStruct((nb, seq, width), bf16),
        scratch_shapes=[tile_f32] * 6 + [tile_bf16] * 3 + [tile_f32] * 6
        + [pltpu.VMEM((3, BAND_BLOCK, BAND_WINDOW), f32)],
        compiler_params=pltpu.CompilerParams(
            dimension_semantics=("arbitrary", "arbitrary"), vmem_limit_bytes=VMEM_LIMIT),
    )(q, k, v)


PIPELINE_DEPTH_B = 1


def _mixer_b_kernel(lam_ref, q_ref, k_ref, v_ref, g_ref, o_ref, s_even, s_odd, m_fin, v_ones,
                    *, out_scale, q_tiles, n_tiles):
    step = pl.program_id(0)
    n_q, seq = s_even.shape[1:]
    kc = KEY_CHUNK_B

    @pl.when(step == 0)
    def _():
        for ref in (s_even, s_odd, m_fin):
            ref[...] = jnp.zeros_like(ref)

    value_tile = jnp.clip(step - PIPELINE_DEPTH_B, 0, n_tiles - 1)

    @pl.when(value_tile % q_tiles == 0)
    def _():
        lane = lax.broadcasted_iota(jnp.int32, (seq, LANES), 1)
        v_ones[:, :LANES] = v_ref[...]
        v_ones[:, LANES:] = jnp.where(lane == 0, 1.0, 0.0).astype(bf16)

    def stages(s_w, s_r):
        q = q_ref[...]
        lane = lax.broadcasted_iota(jnp.int32, q.shape, 1)
        first = lane < HEAD_DIM
        zero = jnp.zeros_like(q)
        q_sub = (jnp.where(first, q, zero), jnp.where(first, zero, q))
        nt = (((1,), (1,)), ((), ()))
        m_prev = (m_fin[0], m_fin[1])
        row_max = [None, None]
        acc = [None, None]
        for c in range(seq // kc):
            keys = slice(c * kc, (c + 1) * kc)
            k_c = k_ref[keys, :]
            for sub in range(2):
                s = lax.dot_general(q_sub[sub], k_c, nt, preferred_element_type=f32)
                s_w[sub, :, keys] = s
                part = functools.reduce(
                    jnp.maximum, [s[:, j * LANES:(j + 1) * LANES] for j in range(kc // LANES)])
                row_max[sub] = part if c == 0 else jnp.maximum(row_max[sub], part)
            for sub in range(2):
                e = jnp.concatenate(
                    [jnp.exp2(s_r[sub, :, col:col + LANES] - m_prev[sub]).astype(bf16)
                     for col in range(c * kc, (c + 1) * kc, LANES)], axis=1)
                pv = jnp.dot(e, v_ones[keys, :], preferred_element_type=f32)
                acc[sub] = pv if c == 0 else acc[sub] + pv

        for sub in range(2):
            m_fin[sub] = jnp.broadcast_to(jnp.max(row_max[sub], axis=1, keepdims=True), (n_q, LANES))
        c0 = 1.0 / acc[0][:, LANES:LANES + 1]
        c1 = lam_ref[0] / acc[1][:, LANES:LANES + 1]
        o = acc[0][:, :LANES] * c0 - acc[1][:, :LANES] * c1
        ms = jnp.mean(o * o, axis=-1, keepdims=True)
        o_ref[...] = (o * lax.rsqrt(ms + EPS) * g_ref[...] * out_scale).astype(o_ref.dtype)

    @pl.when(step % 2 == 0)
    def _():
        stages(s_even, s_odd)

    @pl.when(step % 2 == 1)
    def _():
        stages(s_odd, s_even)


def _mixer_b(lam, q, k, v, g, out_scale):
    nb, seq, width = q.shape
    heads = width // LANES
    q_tiles = seq // Q_TILE_B
    n_tiles = nb * heads * q_tiles

    def tile(t):
        return t // (heads * q_tiles), (t // q_tiles) % heads, t % q_tiles

    def score_tile(step):
        return tile(jnp.minimum(step, n_tiles - 1))

    def value_tile(step):
        return tile(jnp.clip(step - PIPELINE_DEPTH_B, 0, n_tiles - 1))

    def q_map(step, lam):
        b, h, i = score_tile(step)
        return b, i, h

    def k_map(step, lam):
        b, h, _ = score_tile(step)
        return b, 0, h

    def v_map(step, lam):
        b, h, _ = value_tile(step)
        return b, 0, h

    def o_map(step, lam):
        b, h, i = value_tile(step)
        return b, i, h

    return pl.pallas_call(
        functools.partial(_mixer_b_kernel, out_scale=out_scale, q_tiles=q_tiles, n_tiles=n_tiles),
        grid_spec=pltpu.PrefetchScalarGridSpec(
            num_scalar_prefetch=1,
            grid=(n_tiles + PIPELINE_DEPTH_B,),
            in_specs=[pl.BlockSpec((None, Q_TILE_B, LANES), q_map),
                      pl.BlockSpec((None, seq, LANES), k_map),
                      pl.BlockSpec((None, seq, LANES), v_map),
                      pl.BlockSpec((1, LANES), lambda step, lam: (0, 0))],
            out_specs=pl.BlockSpec((None, Q_TILE_B, LANES), o_map),
            scratch_shapes=[pltpu.VMEM((2, Q_TILE_B, seq), f32)] * 2
            + [pltpu.VMEM((2, Q_TILE_B, LANES), f32), pltpu.VMEM((seq, 2 * LANES), bf16)],
        ),
        out_shape=jax.ShapeDtypeStruct((nb, seq, width), bf16),
        compiler_params=pltpu.CompilerParams(
            dimension_semantics=("arbitrary",), vmem_limit_bytes=VMEM_LIMIT),
    )(lam, q, k, v, g)


def _outproj_kernel(*refs, n_first):
    (oa_ref, ob_ref, wa_ref, wb_ref, g_ref, wr_ref, x1_ref, cls_t_ref, counts_ref) = refs[-9:]
    n = oa_ref.shape[0]
    x1 = (_read_row_source(refs[:-9], n_first)
          + jnp.dot(oa_ref[...], wa_ref[...], preferred_element_type=f32)
          + jnp.dot(ob_ref[...], wb_ref[...], preferred_element_type=f32))
    x1_ref[...] = x1
    ms = jnp.mean(x1 * x1, axis=-1, keepdims=True)
    h = x1 * lax.rsqrt(ms + EPS) * g_ref[...]
    h_hi = h.astype(bf16)
    h_lo = (h - h_hi.astype(f32)).astype(bf16)
    parts = jnp.dot(jnp.concatenate([h_hi, h_lo], axis=0), wr_ref[...], preferred_element_type=f32)
    logits = (parts[:n, :LANES] + parts[:n, LANES:]) + (parts[n:, :LANES] + parts[n:, LANES:])

    lane = lax.broadcasted_iota(jnp.int32, logits.shape, 1)
    lane_f = lane.astype(f32)

    def first_argmax(vals):
        top = jnp.max(vals, axis=1, keepdims=True)
        first = jnp.min(jnp.where(vals == top, lane_f, float(LANES)), axis=1, keepdims=True)
        return first.astype(jnp.int32)

    g_idx = first_argmax(jnp.where(lane < N_GROUPS, logits, NEG))
    group_first = N_GROUPS + g_idx * EXPERTS_PER_GROUP
    in_group = (lane >= group_first) & (lane < group_first + EXPERTS_PER_GROUP)
    e_logits = jnp.where(in_group, logits, NEG)
    i1 = first_argmax(e_logits)
    i2 = first_argmax(jnp.where(lane == i1, NEG, e_logits))
    lo = jnp.minimum(i1, i2) - group_first
    hi = jnp.maximum(i1, i2) - group_first
    pair = jnp.right_shift(lo * (2 * EXPERTS_PER_GROUP - 1 - lo), 1) + (hi - lo - 1)
    cls = g_idx * PAIRS_PER_GROUP + pair
    cls_lanes = jnp.broadcast_to(cls.astype(f32), logits.shape)
    cls_t_ref[...] = jnp.transpose(cls_lanes)[:SUBLANES, :]

    @pl.when(pl.program_id(0) == 0)
    def _():
        counts_ref[...] = jnp.zeros_like(counts_ref)

    counts_ref[...] += jnp.sum(jnp.where(lane == cls, 1.0, 0.0), axis=0, keepdims=True)


def _outproj(oa, ob, x_sources, wo, g, wr):
    T, D = oa.shape[0], wo.shape[1]
    row_spec = lambda width: pl.BlockSpec((TOKEN_TILE, width), lambda i: (i, 0))
    full = lambda a: pl.BlockSpec(a.shape, lambda i: (0, 0))
    assert WIDTH_A == WIDTH_B
    return pl.pallas_call(
        functools.partial(_outproj_kernel, n_first=x_sources[0].shape[0] // TOKEN_TILE),
        grid=(T // TOKEN_TILE,),
        in_specs=_row_source_specs(x_sources, D)
        + [row_spec(WIDTH_A), row_spec(WIDTH_B),
           pl.BlockSpec((WIDTH_A, D), lambda i: (0, 0)), pl.BlockSpec((WIDTH_B, D), lambda i: (1, 0)),
           full(g), full(wr)],
        out_specs=[row_spec(D),
                   pl.BlockSpec((SUBLANES, TOKEN_TILE), lambda i: (0, i)),
                   pl.BlockSpec((1, LANES), lambda i: (0, 0))],
        out_shape=[jax.ShapeDtypeStruct((T, D), f32),
                   jax.ShapeDtypeStruct((SUBLANES, T), f32),
                   jax.ShapeDtypeStruct((1, LANES), f32)],
        compiler_params=pltpu.CompilerParams(
            dimension_semantics=("arbitrary",), vmem_limit_bytes=VMEM_LIMIT),
    )(*x_sources, oa, ob, wo, wo, g, wr)


def _pair_tables():
    lo, hi = [], []
    for g in range(N_GROUPS):
        for a in range(EXPERTS_PER_GROUP):
            for b in range(a + 1, EXPERTS_PER_GROUP):
                lo.append(g * EXPERTS_PER_GROUP + a)
                hi.append(g * EXPERTS_PER_GROUP + b)
    return np.asarray(lo, np.int32), np.asarray(hi, np.int32)


def _block_plan(counts, n_blocks):
    counts = counts.astype(jnp.int32)
    padded = (counts + MOE_BLOCK - 1) // MOE_BLOCK * MOE_BLOCK
    pad_ends = jnp.cumsum(padded)
    pad_starts = pad_ends - padded
    blk_start = jnp.arange(n_blocks, dtype=jnp.int32) * MOE_BLOCK
    blk_cls = jnp.minimum(jnp.sum(pad_ends[None, :] <= blk_start[:, None], axis=1),
                          N_CLASSES - 1).astype(jnp.int32)
    onehot = (blk_cls[:, None] == jnp.arange(N_CLASSES, dtype=jnp.int32)[None, :]).astype(jnp.int32)
    used = jnp.sum(onehot * (counts - (blk_start[:, None] - pad_starts[None, :])), axis=1)
    blk_valid = jnp.where(blk_start < pad_ends[-1], jnp.clip(used, 0, MOE_BLOCK), 0).astype(jnp.int32)
    lo_tab, hi_tab = _pair_tables()
    blk_ea = jnp.sum(onehot * jnp.asarray(lo_tab)[None, :], axis=1).astype(jnp.int32)
    blk_eb = jnp.sum(onehot * jnp.asarray(hi_tab)[None, :], axis=1).astype(jnp.int32)
    return pad_starts, blk_ea, blk_eb, blk_valid


def _positions_kernel(cls_t_ref, start_ref, pos_ref, base_s):
    n = cls_t_ref.shape[1]

    @pl.when(pl.program_id(0) == 0)
    def _():
        base_s[...] = start_ref[...]

    cls = cls_t_ref[0:1, :].astype(jnp.int32)
    onehot = lax.broadcasted_iota(jnp.int32, (LANES, n), 0) == cls
    earlier = (lax.broadcasted_iota(jnp.int32, (n, n), 0)
               < lax.broadcasted_iota(jnp.int32, (n, n), 1))
    onehot_f = jnp.where(onehot, 1.0, 0.0)
    rank = jnp.dot(onehot_f.astype(bf16), jnp.where(earlier, 1.0, 0.0).astype(bf16),
                   preferred_element_type=f32)
    base = base_s[...]
    pos = jnp.sum(jnp.where(onehot, rank + base, 0.0), axis=0, keepdims=True)
    pos_ref[...] = pos.astype(jnp.int32)
    base_s[...] = base + jnp.sum(onehot_f, axis=1, keepdims=True)


def _positions(cls_t, pad_starts):
    T = cls_t.shape[1]
    start_col = jnp.zeros((LANES, 1), f32).at[:N_CLASSES, 0].set(pad_starts.astype(f32))
    return pl.pallas_call(
        _positions_kernel,
        grid=(T // TOKEN_TILE,),
        in_specs=[pl.BlockSpec((SUBLANES, TOKEN_TILE), lambda i: (0, i)),
                  pl.BlockSpec((LANES, 1), lambda i: (0, 0))],
        out_specs=pl.BlockSpec((1, TOKEN_TILE), lambda i: (0, i)),
        out_shape=jax.ShapeDtypeStruct((1, T), jnp.int32),
        scratch_shapes=[pltpu.VMEM((LANES, 1), f32)],
        compiler_params=pltpu.CompilerParams(
            dimension_semantics=("arbitrary",), vmem_limit_bytes=VMEM_LIMIT),
    )(cls_t, start_col)


def _token_copy_loops(make_copy, n):
    def start(r, c):
        make_copy(r).start()
        return c
    lax.fori_loop(0, n, start, 0, unroll=DMA_UNROLL)

    def wait(r, c):
        make_copy(r).wait()
        return c
    lax.fori_loop(0, n, wait, 0, unroll=DMA_UNROLL)


def _dispatch_kernel(pos_ref, x_ref, init_hbm, xs_hbm, stage, sem):
    del init_hbm
    n, D = x_ref.shape
    chunks = D // LANES
    for j in range(chunks):
        stage[pl.ds(j, n, stride=chunks), :] = x_ref[:, j * LANES:(j + 1) * LANES]

    def copy(r):
        src = stage.at[pl.ds(pl.multiple_of(r * chunks, chunks), chunks)]
        dst = xs_hbm.at[pl.ds(pl.multiple_of(pos_ref[0, r] * chunks, chunks), chunks)]
        return pltpu.make_async_copy(src, dst, sem)

    _token_copy_loops(copy, n)


def _dispatch(pos3, x1, n_rows):
    n_tiles, _, tile = pos3.shape
    D = x1.shape[1]
    chunks = D // LANES
    shape = jax.ShapeDtypeStruct((n_rows * chunks, LANES), f32)
    return pl.pallas_call(
        _dispatch_kernel,
        grid=(n_tiles,),
        in_specs=[pl.BlockSpec((None, 1, tile), lambda i: (i, 0, 0), memory_space=pltpu.SMEM),
                  pl.BlockSpec((tile, D), lambda i: (i, 0)),
                  pl.BlockSpec(memory_space=pl.ANY)],
        out_specs=pl.BlockSpec(memory_space=pl.ANY),
        out_shape=shape,
        scratch_shapes=[pltpu.VMEM((tile * chunks, LANES), f32), pltpu.SemaphoreType.DMA],
        input_output_aliases={2: 0},
        compiler_params=pltpu.CompilerParams(
            dimension_semantics=("arbitrary",), vmem_limit_bytes=VMEM_LIMIT),
    )(pos3, x1, jnp.zeros(shape.shape, f32))


def _combine_kernel(pos_ref, ys_hbm, *rest, split):
    outs, (stage, sem) = rest[:-2], rest[-2:]
    n, D = outs[0].shape
    chunks = D // LANES

    def copy(r):
        src = ys_hbm.at[pl.ds(pl.multiple_of(pos_ref[0, r] * chunks, chunks), chunks)]
        dst = stage.at[pl.ds(pl.multiple_of(r * chunks, chunks), chunks)]
        return pltpu.make_async_copy(src, dst, sem)

    _token_copy_loops(copy, n)

    def write(o_ref):
        for j in range(chunks):
            o_ref[:, j * LANES:(j + 1) * LANES] = stage[pl.ds(j, n, stride=chunks), :]

    if split is None:
        write(outs[0])
    else:
        pl.when(pl.program_id(0) < split)(lambda: write(outs[0]))
        pl.when(pl.program_id(0) >= split)(lambda: write(outs[1]))


def _combine(pos3, ys, D, split=None):
    n_tiles, _, tile = pos3.shape
    chunks = D // LANES
    if split is None:
        out_specs = pl.BlockSpec((tile, D), lambda i: (i, 0))
        out_shape = jax.ShapeDtypeStruct((n_tiles * tile, D), f32)
    else:
        out_specs = [pl.BlockSpec((tile, D), lambda i: (jnp.minimum(i, split - 1), 0)),
                     pl.BlockSpec((tile, D), lambda i: (jnp.maximum(i - split, 0), 0))]
        out_shape = [jax.ShapeDtypeStruct((split * tile, D), f32),
                     jax.ShapeDtypeStruct(((n_tiles - split) * tile, D), f32)]
    return pl.pallas_call(
        functools.partial(_combine_kernel, split=split),
        grid=(n_tiles,),
        in_specs=[pl.BlockSpec((None, 1, tile), lambda i: (i, 0, 0), memory_space=pltpu.SMEM),
                  pl.BlockSpec(memory_space=pl.ANY)],
        out_specs=out_specs,
        out_shape=out_shape,
        scratch_shapes=[pltpu.VMEM((tile * chunks, LANES), f32), pltpu.SemaphoreType.DMA],
        compiler_params=pltpu.CompilerParams(
            dimension_semantics=("arbitrary",), vmem_limit_bytes=VMEM_LIMIT),
    )(pos3, ys)


def _experts_kernel(ea_ref, eb_ref, nv_ref, xs_ref, g2_ref, gf_ref, wr_ref,
                    wgu_a_ref, wd_a_ref, wgu_b_ref, wd_b_ref, ys_ref, *, final_norm):
    D = g2_ref.shape[1]
    chunks = D // LANES
    n = xs_ref.shape[0] // chunks
    d_ff = wd_a_ref.shape[0]
    i = pl.program_id(0)
    active = nv_ref[i] > 0

    @pl.when(active)
    def _():
        x = jnp.concatenate([xs_ref[pl.ds(j, n, stride=chunks), :] for j in range(chunks)], axis=1)
        ms = jnp.mean(x * x, axis=-1, keepdims=True)
        h = (x * lax.rsqrt(ms + EPS) * g2_ref[...]).astype(bf16)

        logits = jnp.dot(h, wr_ref[...], preferred_element_type=f32)
        lane = lax.broadcasted_iota(jnp.int32, logits.shape, 1)
        pick = lambda idx: jnp.sum(jnp.where(lane == idx, logits, 0.0), axis=1, keepdims=True)
        ea, eb = ea_ref[i], eb_ref[i]
        l_group = pick(ea // EXPERTS_PER_GROUP)
        p_group = 1.0 / jnp.sum(jnp.where(lane < N_GROUPS, jnp.exp(logits - l_group), 0.0),
                                axis=1, keepdims=True)
        l_a, l_b = pick(N_GROUPS + ea), pick(N_GROUPS + eb)
        gates = (p_group / (1.0 + jnp.exp(l_b - l_a)), p_group / (1.0 + jnp.exp(l_a - l_b)))

        y = x
        for gate, wgu_ref, wd_ref in ((gates[0], wgu_a_ref, wd_a_ref), (gates[1], wgu_b_ref, wd_b_ref)):
            gu = jnp.dot(h, wgu_ref[...], preferred_element_type=f32)
            a = gu[:, :d_ff]
            act = (a * jax.nn.sigmoid(a) * gu[:, d_ff:]).astype(bf16)
            y = y + gate * jnp.dot(act, wd_ref[...], preferred_element_type=f32)
        if final_norm:
            ms = jnp.mean(y * y, axis=-1, keepdims=True)
            y = y * lax.rsqrt(ms + EPS) * gf_ref[...]
        for j in range(chunks):
            ys_ref[pl.ds(j, n, stride=chunks), :] = y[:, j * LANES:(j + 1) * LANES]

    @pl.when(jnp.logical_not(active))
    def _():
        ys_ref[...] = jnp.zeros_like(ys_ref)


def _experts(xs, blk_ea, blk_eb, blk_valid, g2, gf, wr, wgu, wd, final_norm):
    D = g2.shape[1]
    chunks = D // LANES
    d_ff = wd.shape[1]
    blk = MOE_BLOCK * chunks
    idx = lambda f: (lambda i, ea, eb, nv: f(i, ea, eb))
    const = lambda a: pl.BlockSpec(a.shape, idx(lambda i, ea, eb: (0, 0)))
    return pl.pallas_call(
        functools.partial(_experts_kernel, final_norm=final_norm),
        grid_spec=pltpu.PrefetchScalarGridSpec(
            num_scalar_prefetch=3,
            grid=(xs.shape[0] // blk,),
            in_specs=[
                pl.BlockSpec((blk, LANES), idx(lambda i, ea, eb: (i, 0))),
                const(g2), const(gf), const(wr),
                pl.BlockSpec((None, D, 2 * d_ff), idx(lambda i, ea, eb: (ea[i], 0, 0))),
                pl.BlockSpec((None, d_ff, D), idx(lambda i, ea, eb: (ea[i], 0, 0))),
                pl.BlockSpec((None, D, 2 * d_ff), idx(lambda i, ea, eb: (eb[i], 0, 0))),
                pl.BlockSpec((None, d_ff, D), idx(lambda i, ea, eb: (eb[i], 0, 0))),
            ],
            out_specs=pl.BlockSpec((blk, LANES), idx(lambda i, ea, eb: (i, 0))),
        ),
        out_shape=jax.ShapeDtypeStruct(xs.shape, f32),
        compiler_params=pltpu.CompilerParams(
            dimension_semantics=("arbitrary",), vmem_limit_bytes=VMEM_LIMIT),
    )(blk_ea, blk_eb, blk_valid, xs, g2, gf, wr, wgu, wd, wgu, wd)


def _moe(x1, cls_t, counts, g2, gf, wr, wgu, wd, final_norm, split=None):
    T, D = x1.shape
    n_rows = T + N_CLASSES * MOE_BLOCK
    pad_starts, blk_ea, blk_eb, blk_valid = _block_plan(counts[0, :N_CLASSES], n_rows // MOE_BLOCK)
    pos3 = _positions(cls_t, pad_starts).reshape(T // TOKEN_TILE, 1, TOKEN_TILE)
    xs = _dispatch(pos3, x1, n_rows)
    ys = _experts(xs, blk_ea, blk_eb, blk_valid, g2, gf, wr, wgu, wd, final_norm)
    return _combine(pos3, ys, D, split)


def kernel(x_prompt, x_sample, norm1_g, w_in, lambda_q1, lambda_k1, lambda_q2, lambda_k2, subln_g,
           w_out, norm2_g, w_router_group, w_router_expert, w_gate, w_up, w_down, final_norm_g):
    n_prompt, seq, D = x_prompt.shape
    assert x_sample.shape[1:] == (seq, D)
    nb = n_prompt + x_sample.shape[0]
    depth = w_in.shape[0]
    x_sources = (x_prompt.reshape(n_prompt * seq, D), x_sample.reshape((nb - n_prompt) * seq, D))
    cos_np, sin_np = _rope_tables(seq)
    cos, sin = jnp.asarray(cos_np), jnp.asarray(sin_np)
    row = lambda v: v.reshape(1, -1).astype(f32)

    for l in range(depth):
        qa, ka, va, qb, kb, vb = _inproj(x_sources, row(norm1_g[l]), w_in[l].astype(bf16), cos, sin,
                                         seq)
        to_seq = lambda t: t.reshape(nb, seq, t.shape[-1])
        oa = _mixer_a(to_seq(qa), to_seq(ka), to_seq(va))
        lam_init = 0.8 - 0.6 * math.exp(-0.3 * l)
        lam = (jnp.exp(jnp.sum(lambda_q1[l].astype(f32) * lambda_k1[l].astype(f32)))
               - jnp.exp(jnp.sum(lambda_q2[l].astype(f32) * lambda_k2[l].astype(f32)))
               + lam_init).reshape(1)
        ob = _mixer_b(lam, to_seq(qb), to_seq(kb), to_seq(vb), row(subln_g[l]), 1.0 - lam_init)

        w_router = jnp.concatenate([w_router_group[l], w_router_expert[l]], axis=1).astype(f32)
        w_router = jnp.pad(w_router, ((0, 0), (0, LANES - w_router.shape[1])))
        wr_hi = w_router.astype(bf16)
        wr_lo = (w_router - wr_hi.astype(f32)).astype(bf16)
        x1, cls_t, counts = _outproj(oa.reshape(nb * seq, WIDTH_A), ob.reshape(nb * seq, WIDTH_B),
                                     x_sources, w_out[l].astype(bf16), row(norm2_g[l]),
                                     jnp.concatenate([wr_hi, wr_lo], axis=1))

        wgu = jnp.concatenate([w_gate[l], w_up[l]], axis=-1).astype(bf16)
        last = l == depth - 1
        y = _moe(x1, cls_t, counts, row(norm2_g[l]), row(final_norm_g), wr_hi, wgu,
                 w_down[l].astype(bf16), final_norm=last,
                 split=n_prompt * seq // TOKEN_TILE if last else None)
        x_sources = y if last else (y,)

    y_prompt, y_sample = x_sources
    return y_prompt.reshape(n_prompt, seq, D), y_sample.reshape(nb - n_prompt, seq, D)
```

```python
import functools
import math

import numpy as np
import jax
import jax.numpy as jnp
from jax import lax
from jax.experimental import pallas as pl
from jax.experimental.pallas import tpu as pltpu

HEAD_DIM = 64
N_HEADS_A = 8
WIDTH_A = N_HEADS_A * HEAD_DIM
DIL_PATTERNS = ((128, 1), (512, 4), (2048, 16))
N_HEADS_B = 4
QK_WIDTH_B = N_HEADS_B * 2 * HEAD_DIM
V_DIM_B = 2 * HEAD_DIM
WIDTH_B = N_HEADS_B * V_DIM_B
IN_COLS = 3 * WIDTH_A + 2 * QK_WIDTH_B + WIDTH_B
ROPE_THETA = 10000.0
EPS = 1e-6
N_GROUPS = 4
EXPERTS_PER_GROUP = 8
N_EXPERTS = N_GROUPS * EXPERTS_PER_GROUP
PAIRS_PER_GROUP = EXPERTS_PER_GROUP * (EXPERTS_PER_GROUP - 1) // 2
N_CLASSES = N_GROUPS * PAIRS_PER_GROUP
NEG = -1e30

LANES = 128
SUBLANES = 8
TOKEN_TILE = 512
Q_TILE_B = 256
KEY_CHUNK_B = 1024
BAND_BLOCK = 128
BAND_RADIUS = 64
BAND_WINDOW = BAND_BLOCK + 2 * BAND_RADIUS
BAND_UNROLL = 16
MOE_BLOCK = 256
DMA_UNROLL = 8
VMEM_LIMIT = 56 * 1024 * 1024

QK_SCALE = HEAD_DIM ** -0.5 * math.log2(math.e)

f32 = jnp.float32
bf16 = jnp.bfloat16


def _rope_tables(seq):
    half = HEAD_DIM // 2
    inv = ROPE_THETA ** (-np.arange(half, dtype=np.float64) / half)
    lane = np.arange(LANES)
    ang = np.arange(seq, dtype=np.float64)[:, None] * inv[lane % half][None, :]
    sign = np.where((lane % HEAD_DIM) < half, -1.0, 1.0)[None, :]
    return np.cos(ang).astype(np.float32), (np.sin(ang) * sign).astype(np.float32)


def _row_source_specs(sources, width):
    if len(sources) == 1:
        return [pl.BlockSpec((TOKEN_TILE, width), lambda i: (i, 0))]
    n_first = sources[0].shape[0] // TOKEN_TILE
    return [pl.BlockSpec((TOKEN_TILE, width), lambda i: (jnp.minimum(i, n_first - 1), 0)),
            pl.BlockSpec((TOKEN_TILE, width), lambda i: (jnp.maximum(i - n_first, 0), 0))]


def _read_row_source(refs, n_first):
    if len(refs) == 1:
        return refs[0][...]
    return jnp.where(pl.program_id(0) < n_first, refs[0][...], refs[1][...])


def _inproj_kernel(*refs, n_first):
    (g_ref, w_ref, cos_ref, sin_ref, qa_ref, ka_ref, va_ref, qb_ref, kb_ref, vb_ref) = refs[-10:]
    x = _read_row_source(refs[:-10], n_first)
    ms = jnp.mean(x * x, axis=-1, keepdims=True)
    h = (x * lax.rsqrt(ms + EPS) * g_ref[...]).astype(bf16)
    p = jnp.dot(h, w_ref[...], preferred_element_type=f32)
    cos = cos_ref[...]
    sin = sin_ref[...]
    lane = lax.broadcasted_iota(jnp.int32, cos.shape, 1)
    first_half = (lane & (HEAD_DIM - 1)) < (HEAD_DIM // 2)

    def rope(t):
        rot = jnp.where(first_half, pltpu.roll(t, LANES - HEAD_DIM // 2, 1),
                        pltpu.roll(t, HEAD_DIM // 2, 1))
        return t * cos + rot * sin

    plan = ((qa_ref, 0, True, QK_SCALE), (ka_ref, WIDTH_A, True, None),
            (va_ref, 2 * WIDTH_A, False, None),
            (qb_ref, 3 * WIDTH_A, True, QK_SCALE), (kb_ref, 3 * WIDTH_A + QK_WIDTH_B, True, None),
            (vb_ref, 3 * WIDTH_A + 2 * QK_WIDTH_B, False, None))
    for ref, off, rotary, scale in plan:
        for c in range(ref.shape[1] // LANES):
            t = p[:, off + c * LANES: off + (c + 1) * LANES]
            if rotary:
                t = rope(t)
            if scale is not None:
                t = t * scale
            ref[:, c * LANES:(c + 1) * LANES] = t.astype(bf16)


def _inproj(x_sources, g, w, cos, sin, seq):
    T = sum(x.shape[0] for x in x_sources)
    D = x_sources[0].shape[1]
    tiles_per_seq = seq // TOKEN_TILE
    out = jax.ShapeDtypeStruct((T, WIDTH_A), bf16)
    row_spec = lambda width: pl.BlockSpec((TOKEN_TILE, width), lambda i: (i, 0))
    return pl.pallas_call(
        functools.partial(_inproj_kernel, n_first=x_sources[0].shape[0] // TOKEN_TILE),
        grid=(T // TOKEN_TILE,),
        in_specs=_row_source_specs(x_sources, D) + [
            pl.BlockSpec((1, D), lambda i: (0, 0)),
            pl.BlockSpec((D, IN_COLS), lambda i: (0, 0)),
            pl.BlockSpec((TOKEN_TILE, LANES), lambda i: (i % tiles_per_seq, 0)),
            pl.BlockSpec((TOKEN_TILE, LANES), lambda i: (i % tiles_per_seq, 0)),
        ],
        out_specs=[row_spec(WIDTH_A)] * 6,
        out_shape=[out] * 6,
        compiler_params=pltpu.CompilerParams(
            dimension_semantics=("arbitrary",), vmem_limit_bytes=VMEM_LIMIT),
    )(*x_sources, g, w, cos, sin)


def _band_block(q, k, v, bias):
    nq = q.shape[0]
    lane = lax.broadcasted_iota(jnp.int32, (nq, LANES), 1)
    head0 = lane < HEAD_DIM
    parts = []
    for sel in (head0, jnp.logical_not(head0)):
        qm = jnp.where(sel, q, jnp.zeros_like(q))
        s = lax.dot_general(qm, k, (((1,), (1,)), ((), ())), preferred_element_type=f32) + bias
        m = jnp.max(s, axis=1, keepdims=True)
        e = jnp.exp2(s - m)
        l = jnp.sum(e, axis=1, keepdims=True)
        o = jnp.dot(e.astype(bf16), v, preferred_element_type=f32)
        parts.append((o, m, l))
    (o0, m0, l0), (o1, m1, l1) = parts
    return (jnp.where(head0, o0, o1),
            jnp.where(head0, jnp.broadcast_to(m0, (nq, LANES)), jnp.broadcast_to(m1, (nq, LANES))),
            jnp.where(head0, jnp.broadcast_to(l0, (nq, LANES)), jnp.broadcast_to(l1, (nq, LANES))))


def _mixer_a_kernel(q_ref, k_ref, v_ref, o_ref, qf, kf, vf, q2, k2, v2, qd, kd, vd, oc, mc, lc,
                    acc_s, m_s, l_s, bias_s):
    seq = q_ref.shape[0]
    n_blocks = seq // BAND_BLOCK

    rel = (lax.broadcasted_iota(jnp.int32, (BAND_BLOCK, BAND_WINDOW), 0)
           - lax.broadcasted_iota(jnp.int32, (BAND_BLOCK, BAND_WINDOW), 1))
    for j in range(3):
        bias_s[j] = jnp.where(jnp.abs(rel + j * BAND_RADIUS) <= BAND_RADIUS, 0.0, NEG)

    dils = [d for _, d in DIL_PATTERNS]

    def refine(i):
        ratio = dils[i] // dils[i - 1]
        len_prev, len_cur = seq // dils[i - 1], seq // dils[i]
        return [(pl.ds(s * len_prev + r, len_cur, stride=ratio),
                 pl.ds((s * ratio + r) * len_cur, len_cur))
                for s in range(dils[i - 1]) for r in range(ratio)]

    f32_bufs = ((qf, kf, vf), (q2, k2, v2))
    state = (acc_s, m_s, l_s)
    for i, dil in enumerate(dils):
        cls_len = seq // dil
        blocks_per_cls = cls_len // BAND_BLOCK
        if i == 0:
            src, dst = (q_ref, k_ref, v_ref), state
        else:
            prev_f32, cur_f32 = f32_bufs[(i - 1) % 2], f32_bufs[i % 2]
            if i == 1:
                for buf, ref in zip(prev_f32, (q_ref, k_ref, v_ref)):
                    buf[...] = ref[...].astype(f32)
            for strided, rows in refine(i):
                for major, cur, prev in zip((qd, kd, vd), cur_f32, prev_f32):
                    t = prev[strided, :]
                    major[rows, :] = t.astype(bf16)
                    if i + 1 < len(dils):
                        cur[rows, :] = t
            src, dst = (qd, kd, vd), (oc, mc, lc)

        def body(it, carry, src=src, dst=dst, cls_len=cls_len, blocks_per_cls=blocks_per_cls):
            for u in range(BAND_UNROLL):
                b = it * BAND_UNROLL + u
                blk = b % blocks_per_cls
                base = (b // blocks_per_cls) * cls_len
                q0 = blk * BAND_BLOCK
                ws = jnp.clip(q0 - BAND_RADIUS, 0, cls_len - BAND_WINDOW)
                q_rows = pl.ds(pl.multiple_of(base + q0, BAND_BLOCK), BAND_BLOCK)
                k_rows = pl.ds(pl.multiple_of(base + ws, BAND_RADIUS), BAND_WINDOW)
                o, m, l = _band_block(src[0][q_rows, :], src[1][k_rows, :], src[2][k_rows, :],
                                      bias_s[(q0 - ws) // BAND_RADIUS])
                dst[0][q_rows, :] = o
                dst[1][q_rows, :] = m
                dst[2][q_rows, :] = l
            return carry

        lax.fori_loop(0, n_blocks // BAND_UNROLL, body, 0)

        if i > 0:
            new_state = f32_bufs[(i - 1) % 2]
            for strided, rows in refine(i):
                m_old = state[1][strided, :]
                m_pat = mc[rows, :]
                m_new = jnp.maximum(m_old, m_pat)
                a_old = jnp.exp2(m_old - m_new)
                a_pat = jnp.exp2(m_pat - m_new)
                new_state[0][rows, :] = state[0][strided, :] * a_old + oc[rows, :] * a_pat
                new_state[2][rows, :] = state[2][strided, :] * a_old + lc[rows, :] * a_pat
                new_state[1][rows, :] = m_new
            state = new_state

    hops = (acc_s, m_s, l_s)
    assert len(dils) == len(hops) and state[0] is not acc_s
    hops[0][...] = state[0][...] / state[2][...]
    for hop, i in enumerate(reversed(range(1, len(dils)))):
        for strided, rows in refine(i):
            hops[hop + 1][strided, :] = hops[hop][rows, :]
    o_ref[...] = hops[len(dils) - 1][...].astype(o_ref.dtype)


def _mixer_a(q, k, v):
    nb, seq, width = q.shape
    spec = pl.BlockSpec((None, seq, LANES), lambda b, p: (b, 0, p))
    tile_f32 = pltpu.VMEM((seq, LANES), f32)
    tile_bf16 = pltpu.VMEM((seq, LANES), bf16)
    return pl.pallas_call(
        _mixer_a_kernel,
        grid=(nb, width // LANES),
        in_specs=[spec, spec, spec],
        out_specs=spec,
        out_shape=jax.ShapeDtypeStruct((nb, seq, width), bf16),
        scratch_shapes=[tile_f32] * 6 + [tile_bf16] * 3 + [tile_f32] * 6
        + [pltpu.VMEM((3, BAND_BLOCK, BAND_WINDOW), f32)],
        compiler_params=pltpu.CompilerParams(
            dimension_semantics=("arbitrary", "arbitrary"), vmem_limit_bytes=VMEM_LIMIT),
    )(q, k, v)


PIPELINE_DEPTH_B = 1


def _mixer_b_kernel(lam_ref, q_ref, k_ref, v_ref, g_ref, o_ref, s_even, s_odd, m_fin, v_ones,
                    *, out_scale, q_tiles, n_tiles):
    step = pl.program_id(0)
    n_q, seq = s_even.shape[1:]
    kc = KEY_CHUNK_B

    @pl.when(step == 0)
    def _():
        for ref in (s_even, s_odd, m_fin):
            ref[...] = jnp.zeros_like(ref)

    value_tile = jnp.clip(step - PIPELINE_DEPTH_B, 0, n_tiles - 1)

    @pl.when(value_tile % q_tiles == 0)
    def _():
        lane = lax.broadcasted_iota(jnp.int32, (seq, LANES), 1)
        v_ones[:, :LANES] = v_ref[...]
        v_ones[:, LANES:] = jnp.where(lane == 0, 1.0, 0.0).astype(bf16)

    def stages(s_w, s_r):
        q = q_ref[...]
        lane = lax.broadcasted_iota(jnp.int32, q.shape, 1)
        first = lane < HEAD_DIM
        zero = jnp.zeros_like(q)
        q_sub = (jnp.where(first, q, zero), jnp.where(first, zero, q))
        nt = (((1,), (1,)), ((), ()))
        m_prev = (m_fin[0], m_fin[1])
        row_max = [None, None]
        acc = [None, None]
        for c in range(seq // kc):
            keys = slice(c * kc, (c + 1) * kc)
            k_c = k_ref[keys, :]
            for sub in range(2):
                s = lax.dot_general(q_sub[sub], k_c, nt, preferred_element_type=f32)
                s_w[sub, :, keys] = s
                part = functools.reduce(
                    jnp.maximum, [s[:, j * LANES:(j + 1) * LANES] for j in range(kc // LANES)])
                row_max[sub] = part if c == 0 else jnp.maximum(row_max[sub], part)
            for sub in range(2):
                e = jnp.concatenate(
                    [jnp.exp2(s_r[sub, :, col:col + LANES] - m_prev[sub]).astype(bf16)
                     for col in range(c * kc, (c + 1) * kc, LANES)], axis=1)
                pv = jnp.dot(e, v_ones[keys, :], preferred_element_type=f32)
                acc[sub] = pv if c == 0 else acc[sub] + pv

        for sub in range(2):
            m_fin[sub] = jnp.broadcast_to(jnp.max(row_max[sub], axis=1, keepdims=True), (n_q, LANES))
        c0 = 1.0 / acc[0][:, LANES:LANES + 1]
        c1 = lam_ref[0] / acc[1][:, LANES:LANES + 1]
        o = acc[0][:, :LANES] * c0 - acc[1][:, :LANES] * c1
        ms = jnp.mean(o * o, axis=-1, keepdims=True)
        o_ref[...] = (o * lax.rsqrt(ms + EPS) * g_ref[...] * out_scale).astype(o_ref.dtype)

    @pl.when(step % 2 == 0)
    def _():
        stages(s_even, s_odd)

    @pl.when(step % 2 == 1)
    def _():
        stages(s_odd, s_even)


def _mixer_b(lam, q, k, v, g, out_scale):
    nb, seq, width = q.shape
    heads = width // LANES
    q_tiles = seq // Q_TILE_B
    n_tiles = nb * heads * q_tiles

    def tile(t):
        return t // (heads * q_tiles), (t // q_tiles) % heads, t % q_tiles

    def score_tile(step):
        return tile(jnp.minimum(step, n_tiles - 1))

    def value_tile(step):
        return tile(jnp.clip(step - PIPELINE_DEPTH_B, 0, n_tiles - 1))

    def q_map(step, lam):
        b, h, i = score_tile(step)
        return b, i, h

    def k_map(step, lam):
        b, h, _ = score_tile(step)
        return b, 0, h

    def v_map(step, lam):
        b, h, _ = value_tile(step)
        return b, 0, h

    def o_map(step, lam):
        b, h, i = value_tile(step)
        return b, i, h

    return pl.pallas_call(
        functools.partial(_mixer_b_kernel, out_scale=out_scale, q_tiles=q_tiles, n_tiles=n_tiles),
        grid_spec=pltpu.PrefetchScalarGridSpec(
            num_scalar_prefetch=1,
            grid=(n_tiles + PIPELINE_DEPTH_B,),
            in_specs=[pl.BlockSpec((None, Q_TILE_B, LANES), q_map),
                      pl.BlockSpec((None, seq, LANES), k_map),
                      pl.BlockSpec((None, seq, LANES), v_map),
                      pl.BlockSpec((1, LANES), lambda step, lam: (0, 0))],
            out_specs=pl.BlockSpec((None, Q_TILE_B, LANES), o_map),
            scratch_shapes=[pltpu.VMEM((2, Q_TILE_B, seq), f32)] * 2
            + [pltpu.VMEM((2, Q_TILE_B, LANES), f32), pltpu.VMEM((seq, 2 * LANES), bf16)],
        ),
        out_shape=jax.ShapeDtypeStruct((nb, seq, width), bf16),
        compiler_params=pltpu.CompilerParams(
            dimension_semantics=("arbitrary",), vmem_limit_bytes=VMEM_LIMIT),
    )(lam, q, k, v, g)


def _outproj_kernel(*refs, n_first):
    (oa_ref, ob_ref, wa_ref, wb_ref, g_ref, wr_ref, x1_ref, cls_t_ref, counts_ref) = refs[-9:]
    n = oa_ref.shape[0]
    x1 = (_read_row_source(refs[:-9], n_first)
          + jnp.dot(oa_ref[...], wa_ref[...], preferred_element_type=f32)
          + jnp.dot(ob_ref[...], wb_ref[...], preferred_element_type=f32))
    x1_ref[...] = x1
    ms = jnp.mean(x1 * x1, axis=-1, keepdims=True)
    h = x1 * lax.rsqrt(ms + EPS) * g_ref[...]
    h_hi = h.astype(bf16)
    h_lo = (h - h_hi.astype(f32)).astype(bf16)
    parts = jnp.dot(jnp.concatenate([h_hi, h_lo], axis=0), wr_ref[...], preferred_element_type=f32)
    logits = (parts[:n, :LANES] + parts[:n, LANES:]) + (parts[n:, :LANES] + parts[n:, LANES:])

    lane = lax.broadcasted_iota(jnp.int32, logits.shape, 1)
    lane_f = lane.astype(f32)

    def first_argmax(vals):
        top = jnp.max(vals, axis=1, keepdims=True)
        first = jnp.min(jnp.where(vals == top, lane_f, float(LANES)), axis=1, keepdims=True)
        return first.astype(jnp.int32)

    g_idx = first_argmax(jnp.where(lane < N_GROUPS, logits, NEG))
    group_first = N_GROUPS + g_idx * EXPERTS_PER_GROUP
    in_group = (lane >= group_first) & (lane < group_first + EXPERTS_PER_GROUP)
    e_logits = jnp.where(in_group, logits, NEG)
    i1 = first_argmax(e_logits)
    i2 = first_argmax(jnp.where(lane == i1, NEG, e_logits))
    lo = jnp.minimum(i1, i2) - group_first
    hi = jnp.maximum(i1, i2) - group_first
    pair = jnp.right_shift(lo * (2 * EXPERTS_PER_GROUP - 1 - lo), 1) + (hi - lo - 1)
    cls = g_idx * PAIRS_PER_GROUP + pair
    cls_lanes = jnp.broadcast_to(cls.astype(f32), logits.shape)
    cls_t_ref[...] = jnp.transpose(cls_lanes)[:SUBLANES, :]

    @pl.when(pl.program_id(0) == 0)
    def _():
        counts_ref[...] = jnp.zeros_like(counts_ref)

    counts_ref[...] += jnp.sum(jnp.where(lane == cls, 1.0, 0.0), axis=0, keepdims=True)


def _outproj(oa, ob, x_sources, wo, g, wr):
    T, D = oa.shape[0], wo.shape[1]
    row_spec = lambda width: pl.BlockSpec((TOKEN_TILE, width), lambda i: (i, 0))
    full = lambda a: pl.BlockSpec(a.shape, lambda i: (0, 0))
    assert WIDTH_A == WIDTH_B
    return pl.pallas_call(
        functools.partial(_outproj_kernel, n_first=x_sources[0].shape[0] // TOKEN_TILE),
        grid=(T // TOKEN_TILE,),
        in_specs=_row_source_specs(x_sources, D)
        + [row_spec(WIDTH_A), row_spec(WIDTH_B),
           pl.BlockSpec((WIDTH_A, D), lambda i: (0, 0)), pl.BlockSpec((WIDTH_B, D), lambda i: (1, 0)),
           full(g), full(wr)],
        out_specs=[row_spec(D),
                   pl.BlockSpec((SUBLANES, TOKEN_TILE), lambda i: (0, i)),
                   pl.BlockSpec((1, LANES), lambda i: (0, 0))],
        out_shape=[jax.ShapeDtypeStruct((T, D), f32),
                   jax.ShapeDtypeStruct((SUBLANES, T), f32),
                   jax.ShapeDtypeStruct((1, LANES), f32)],
        compiler_params=pltpu.CompilerParams(
            dimension_semantics=("arbitrary",), vmem_limit_bytes=VMEM_LIMIT),
    )(*x_sources, oa, ob, wo, wo, g, wr)


def _pair_tables():
    lo, hi = [], []
    for g in range(N_GROUPS):
        for a in range(EXPERTS_PER_GROUP):
            for b in range(a + 1, EXPERTS_PER_GROUP):
                lo.append(g * EXPERTS_PER_GROUP + a)
                hi.append(g * EXPERTS_PER_GROUP + b)
    return np.asarray(lo, np.int32), np.asarray(hi, np.int32)


def _block_plan(counts, n_blocks):
    counts = counts.astype(jnp.int32)
    padded = (counts + MOE_BLOCK - 1) // MOE_BLOCK * MOE_BLOCK
    pad_ends = jnp.cumsum(padded)
    pad_starts = pad_ends - padded
    blk_start = jnp.arange(n_blocks, dtype=jnp.int32) * MOE_BLOCK
    blk_cls = jnp.minimum(jnp.sum(pad_ends[None, :] <= blk_start[:, None], axis=1),
                          N_CLASSES - 1).astype(jnp.int32)
    onehot = (blk_cls[:, None] == jnp.arange(N_CLASSES, dtype=jnp.int32)[None, :]).astype(jnp.int32)
    used = jnp.sum(onehot * (counts - (blk_start[:, None] - pad_starts[None, :])), axis=1)
    blk_valid = jnp.where(blk_start < pad_ends[-1], jnp.clip(used, 0, MOE_BLOCK), 0).astype(jnp.int32)
    lo_tab, hi_tab = _pair_tables()
    blk_ea = jnp.sum(onehot * jnp.asarray(lo_tab)[None, :], axis=1).astype(jnp.int32)
    blk_eb = jnp.sum(onehot * jnp.asarray(hi_tab)[None, :], axis=1).astype(jnp.int32)
    return pad_starts, blk_ea, blk_eb, blk_valid


def _positions_kernel(cls_t_ref, start_ref, pos_ref, base_s):
    n = cls_t_ref.shape[1]

    @pl.when(pl.program_id(0) == 0)
    def _():
        base_s[...] = start_ref[...]

    cls = cls_t_ref[0:1, :].astype(jnp.int32)
    onehot = lax.broadcasted_iota(jnp.int32, (LANES, n), 0) == cls
    earlier = (lax.broadcasted_iota(jnp.int32, (n, n), 0)
               < lax.broadcasted_iota(jnp.int32, (n, n), 1))
    onehot_f = jnp.where(onehot, 1.0, 0.0)
    rank = jnp.dot(onehot_f.astype(bf16), jnp.where(earlier, 1.0, 0.0).astype(bf16),
                   preferred_element_type=f32)
    base = base_s[...]
    pos = jnp.sum(jnp.where(onehot, rank + base, 0.0), axis=0, keepdims=True)
    pos_ref[...] = pos.astype(jnp.int32)
    base_s[...] = base + jnp.sum(onehot_f, axis=1, keepdims=True)


def _positions(cls_t, pad_starts):
    T = cls_t.shape[1]
    start_col = jnp.zeros((LANES, 1), f32).at[:N_CLASSES, 0].set(pad_starts.astype(f32))
    return pl.pallas_call(
        _positions_kernel,
        grid=(T // TOKEN_TILE,),
        in_specs=[pl.BlockSpec((SUBLANES, TOKEN_TILE), lambda i: (0, i)),
                  pl.BlockSpec((LANES, 1), lambda i: (0, 0))],
        out_specs=pl.BlockSpec((1, TOKEN_TILE), lambda i: (0, i)),
        out_shape=jax.ShapeDtypeStruct((1, T), jnp.int32),
        scratch_shapes=[pltpu.VMEM((LANES, 1), f32)],
        compiler_params=pltpu.CompilerParams(
            dimension_semantics=("arbitrary",), vmem_limit_bytes=VMEM_LIMIT),
    )(cls_t, start_col)


def _token_copy_loops(make_copy, n):
    def start(r, c):
        make_copy(r).start()
        return c
    lax.fori_loop(0, n, start, 0, unroll=DMA_UNROLL)

    def wait(r, c):
        make_copy(r).wait()
        return c
    lax.fori_loop(0, n, wait, 0, unroll=DMA_UNROLL)


def _dispatch_kernel(pos_ref, x_ref, init_hbm, xs_hbm, stage, sem):
    del init_hbm
    n, D = x_ref.shape
    chunks = D // LANES
    for j in range(chunks):
        stage[pl.ds(j, n, stride=chunks), :] = x_ref[:, j * LANES:(j + 1) * LANES]

    def copy(r):
        src = stage.at[pl.ds(pl.multiple_of(r * chunks, chunks), chunks)]
        dst = xs_hbm.at[pl.ds(pl.multiple_of(pos_ref[0, r] * chunks, chunks), chunks)]
        return pltpu.make_async_copy(src, dst, sem)

    _token_copy_loops(copy, n)


def _dispatch(pos3, x1, n_rows):
    n_tiles, _, tile = pos3.shape
    D = x1.shape[1]
    chunks = D // LANES
    shape = jax.ShapeDtypeStruct((n_rows * chunks, LANES), f32)
    return pl.pallas_call(
        _dispatch_kernel,
        grid=(n_tiles,),
        in_specs=[pl.BlockSpec((None, 1, tile), lambda i: (i, 0, 0), memory_space=pltpu.SMEM),
                  pl.BlockSpec((tile, D), lambda i: (i, 0)),
                  pl.BlockSpec(memory_space=pl.ANY)],
        out_specs=pl.BlockSpec(memory_space=pl.ANY),
        out_shape=shape,
        scratch_shapes=[pltpu.VMEM((tile * chunks, LANES), f32), pltpu.SemaphoreType.DMA],
        input_output_aliases={2: 0},
        compiler_params=pltpu.CompilerParams(
            dimension_semantics=("arbitrary",), vmem_limit_bytes=VMEM_LIMIT),
    )(pos3, x1, jnp.zeros(shape.shape, f32))


def _combine_kernel(pos_ref, ys_hbm, *rest, split):
    outs, (stage, sem) = rest[:-2], rest[-2:]
    n, D = outs[0].shape
    chunks = D // LANES

    def copy(r):
        src = ys_hbm.at[pl.ds(pl.multiple_of(pos_ref[0, r] * chunks, chunks), chunks)]
        dst = stage.at[pl.ds(pl.multiple_of(r * chunks, chunks), chunks)]
        return pltpu.make_async_copy(src, dst, sem)

    _token_copy_loops(copy, n)

    def write(o_ref):
        for j in range(chunks):
            o_ref[:, j * LANES:(j + 1) * LANES] = stage[pl.ds(j, n, stride=chunks), :]

    if split is None:
        write(outs[0])
    else:
        pl.when(pl.program_id(0) < split)(lambda: write(outs[0]))
        pl.when(pl.program_id(0) >= split)(lambda: write(outs[1]))


def _combine(pos3, ys, D, split=None):
    n_tiles, _, tile = pos3.shape
    chunks = D // LANES
    if split is None:
        out_specs = pl.BlockSpec((tile, D), lambda i: (i, 0))
        out_shape = jax.ShapeDtypeStruct((n_tiles * tile, D), f32)
    else:
        out_specs = [pl.BlockSpec((tile, D), lambda i: (jnp.minimum(i, split - 1), 0)),
                     pl.BlockSpec((tile, D), lambda i: (jnp.maximum(i - split, 0), 0))]
        out_shape = [jax.ShapeDtypeStruct((split * tile, D), f32),
                     jax.ShapeDtypeStruct(((n_tiles - split) * tile, D), f32)]
    return pl.pallas_call(
        functools.partial(_combine_kernel, split=split),
        grid=(n_tiles,),
        in_specs=[pl.BlockSpec((None, 1, tile), lambda i: (i, 0, 0), memory_space=pltpu.SMEM),
                  pl.BlockSpec(memory_space=pl.ANY)],
        out_specs=out_specs,
        out_shape=out_shape,
        scratch_shapes=[pltpu.VMEM((tile * chunks, LANES), f32), pltpu.SemaphoreType.DMA],
        compiler_params=pltpu.CompilerParams(
            dimension_semantics=("arbitrary",), vmem_limit_bytes=VMEM_LIMIT),
    )(pos3, ys)


def _experts_kernel(ea_ref, eb_ref, nv_ref, xs_ref, g2_ref, gf_ref, wr_ref,
                    wg_a_ref, wu_a_ref, wd_a_ref, wg_b_ref, wu_b_ref, wd_b_ref, ys_ref,
                    *, final_norm):
    D = g2_ref.shape[1]
    chunks = D // LANES
    n = xs_ref.shape[0] // chunks
    i = pl.program_id(0)
    active = nv_ref[i] > 0

    @pl.when(active)
    def _():
        x = jnp.concatenate([xs_ref[pl.ds(j, n, stride=chunks), :] for j in range(chunks)], axis=1)
        ms = jnp.mean(x * x, axis=-1, keepdims=True)
        h = (x * lax.rsqrt(ms + EPS) * g2_ref[...]).astype(bf16)

        logits = jnp.dot(h, wr_ref[...], preferred_element_type=f32)
        lane = lax.broadcasted_iota(jnp.int32, logits.shape, 1)
        pick = lambda idx: jnp.sum(jnp.where(lane == idx, logits, 0.0), axis=1, keepdims=True)
        ea, eb = ea_ref[i], eb_ref[i]
        l_group = pick(ea // EXPERTS_PER_GROUP)
        p_group = 1.0 / jnp.sum(jnp.where(lane < N_GROUPS, jnp.exp(logits - l_group), 0.0),
                                axis=1, keepdims=True)
        l_a, l_b = pick(N_GROUPS + ea), pick(N_GROUPS + eb)
        gates = (p_group / (1.0 + jnp.exp(l_b - l_a)), p_group / (1.0 + jnp.exp(l_a - l_b)))

        y = x
        for gate, wg_ref, wu_ref, wd_ref in ((gates[0], wg_a_ref, wu_a_ref, wd_a_ref),
                                             (gates[1], wg_b_ref, wu_b_ref, wd_b_ref)):
            a = jnp.dot(h, wg_ref[...], preferred_element_type=f32)
            u = jnp.dot(h, wu_ref[...], preferred_element_type=f32)
            act = (a * jax.nn.sigmoid(a) * u).astype(bf16)
            y = y + gate * jnp.dot(act, wd_ref[...], preferred_element_type=f32)
        if final_norm:
            ms = jnp.mean(y * y, axis=-1, keepdims=True)
            y = y * lax.rsqrt(ms + EPS) * gf_ref[...]
        for j in range(chunks):
            ys_ref[pl.ds(j, n, stride=chunks), :] = y[:, j * LANES:(j + 1) * LANES]

    @pl.when(jnp.logical_not(active))
    def _():
        ys_ref[...] = jnp.zeros_like(ys_ref)


def _experts(xs, blk_ea, blk_eb, blk_valid, g2, gf, wr, wg, wu, wd, final_norm):
    D = g2.shape[1]
    chunks = D // LANES
    d_ff = wd.shape[1]
    blk = MOE_BLOCK * chunks
    idx = lambda f: (lambda i, ea, eb, nv: f(i, ea, eb))
    const = lambda a: pl.BlockSpec(a.shape, idx(lambda i, ea, eb: (0, 0)))
    in_w = lambda pick: pl.BlockSpec((None, D, d_ff), idx(lambda i, ea, eb: (pick(ea, eb)[i], 0, 0)))
    out_w = lambda pick: pl.BlockSpec((None, d_ff, D), idx(lambda i, ea, eb: (pick(ea, eb)[i], 0, 0)))
    lo, hi = (lambda ea, eb: ea), (lambda ea, eb: eb)
    return pl.pallas_call(
        functools.partial(_experts_kernel, final_norm=final_norm),
        grid_spec=pltpu.PrefetchScalarGridSpec(
            num_scalar_prefetch=3,
            grid=(xs.shape[0] // blk,),
            in_specs=[
                pl.BlockSpec((blk, LANES), idx(lambda i, ea, eb: (i, 0))),
                const(g2), const(gf), const(wr),
                in_w(lo), in_w(lo), out_w(lo), in_w(hi), in_w(hi), out_w(hi),
            ],
            out_specs=pl.BlockSpec((blk, LANES), idx(lambda i, ea, eb: (i, 0))),
        ),
        out_shape=jax.ShapeDtypeStruct(xs.shape, f32),
        compiler_params=pltpu.CompilerParams(
            dimension_semantics=("arbitrary",), vmem_limit_bytes=VMEM_LIMIT),
    )(blk_ea, blk_eb, blk_valid, xs, g2, gf, wr, wg, wu, wd, wg, wu, wd)


def _moe(x1, cls_t, counts, g2, gf, wr, wg, wu, wd, final_norm, split=None):
    T, D = x1.shape
    n_rows = T + N_CLASSES * MOE_BLOCK
    pad_starts, blk_ea, blk_eb, blk_valid = _block_plan(counts[0, :N_CLASSES], n_rows // MOE_BLOCK)
    pos3 = _positions(cls_t, pad_starts).reshape(T // TOKEN_TILE, 1, TOKEN_TILE)
    xs = _dispatch(pos3, x1, n_rows)
    ys = _experts(xs, blk_ea, blk_eb, blk_valid, g2, gf, wr, wg, wu, wd, final_norm)
    return _combine(pos3, ys, D, split)


def kernel(x_prompt, x_sample, norm1_g, w_in, lambda_q1, lambda_k1, lambda_q2, lambda_k2, subln_g,
           w_out, norm2_g, w_router_group, w_router_expert, w_gate, w_up, w_down, final_norm_g):
    n_prompt, seq, D = x_prompt.shape
    assert x_sample.shape[1:] == (seq, D)
    nb = n_prompt + x_sample.shape[0]
    depth = w_in.shape[0]
    x_sources = (x_prompt.reshape(n_prompt * seq, D), x_sample.reshape((nb - n_prompt) * seq, D))
    cos_np, sin_np = _rope_tables(seq)
    cos, sin = jnp.asarray(cos_np), jnp.asarray(sin_np)
    row = lambda v: v.reshape(1, -1).astype(f32)

    for l in range(depth):
        qa, ka, va, qb, kb, vb = _inproj(x_sources, row(norm1_g[l]), w_in[l].astype(bf16), cos, sin,
                                         seq)
        to_seq = lambda t: t.reshape(nb, seq, t.shape[-1])
        oa = _mixer_a(to_seq(qa), to_seq(ka), to_seq(va))
        lam_init = 0.8 - 0.6 * math.exp(-0.3 * l)
        lam = (jnp.exp(jnp.sum(lambda_q1[l].astype(f32) * lambda_k1[l].astype(f32)))
               - jnp.exp(jnp.sum(lambda_q2[l].astype(f32) * lambda_k2[l].astype(f32)))
               + lam_init).reshape(1)
        ob = _mixer_b(lam, to_seq(qb), to_seq(kb), to_seq(vb), row(subln_g[l]), 1.0 - lam_init)

        w_router = jnp.concatenate([w_router_group[l], w_router_expert[l]], axis=1).astype(f32)
        w_router = jnp.pad(w_router, ((0, 0), (0, LANES - w_router.shape[1])))
        wr_hi = w_router.astype(bf16)
        wr_lo = (w_router - wr_hi.astype(f32)).astype(bf16)
        x1, cls_t, counts = _outproj(oa.reshape(nb * seq, WIDTH_A), ob.reshape(nb * seq, WIDTH_B),
                                     x_sources, w_out[l].astype(bf16), row(norm2_g[l]),
                                     jnp.concatenate([wr_hi, wr_lo], axis=1))

        last = l == depth - 1
        y = _moe(x1, cls_t, counts, row(norm2_g[l]), row(final_norm_g), wr_hi,
                 w_gate[l].astype(bf16), w_up[l].astype(bf16), w_down[l].astype(bf16),
                 final_norm=last,
                 split=n_prompt * seq // TOKEN_TILE if last else None)
        x_sources = y if last else (y,)

    y_prompt, y_sample = x_sources
    return y_prompt.reshape(n_prompt, seq, D), y_sample.reshape(nb - n_prompt, seq, D)
```
